```python
import jax, jax.numpy as jnp
from jax import lax
import numpy as np

D_MODEL = 1024
BATCH = 2
SEQ = 8192
DEPTH = 4
DEC_BATCH = 32
DEC_SEQ = 64
PAST_LEN = 4096

CHUNK = 64
N_EVEN = DEPTH - DEPTH // 2
N_ODD = DEPTH // 2
A_HEADS = 8
A_HEAD_DIM = D_MODEL // 16
A_WIDTH = A_HEADS * A_HEAD_DIM
A_BLOCK = 128
POOL_WINDOWS = (2, 4, 8, 16)
B_GROUPS = len(POOL_WINDOWS)
B_GROUP_DIM = D_MODEL // 8
B_WIDTH = B_GROUPS * B_GROUP_DIM
POOL_HIST = max(POOL_WINDOWS) - 1
C_WIDTH = D_MODEL // 2
C_KERNEL = 31
D_WIDTH = D_MODEL // 2
D_KERNEL = 3
EVEN_IN = 2 * A_WIDTH + B_WIDTH
ODD_IN = 2 * C_WIDTH + 3 * D_WIDTH
MIX_WIDTH = A_WIDTH + B_WIDTH
D_FF = 4 * D_MODEL
EPS = 1e-6

kernel_name = "hybrid_chunkmlp_pool_conformer_shortconv_step"


def rms_norm(x, g):
    xf = x.astype(jnp.float32)
    y = xf * lax.rsqrt(jnp.mean(xf * xf, axis=-1, keepdims=True) + EPS)
    return (y * g.astype(jnp.float32)).astype(x.dtype)


def layer_norm(x, g, b):
    xf = x.astype(jnp.float32)
    mu = jnp.mean(xf, axis=-1, keepdims=True)
    xc = xf - mu
    y = xc * lax.rsqrt(jnp.mean(xc * xc, axis=-1, keepdims=True) + EPS)
    return (y * g.astype(jnp.float32) + b.astype(jnp.float32)).astype(x.dtype)


def causal_dwconv(x_ext, w):
    c = w.shape[1]
    return lax.conv_general_dilated(
        x_ext, w[:, None, :].astype(x_ext.dtype), window_strides=(1,), padding="VALID",
        dimension_numbers=("NWC", "WIO", "NWC"), feature_group_count=c)


def chunk_gating(u, v, w_s, b_s):
    n, t = v.shape[0], v.shape[1]
    nb = -(-t // A_BLOCK)
    pad = nb * A_BLOCK - t
    vp = jnp.pad(v, ((0, 0), (0, pad), (0, 0), (0, 0))).reshape(n, nb, A_BLOCK, A_HEADS, A_HEAD_DIM)
    cidx = jnp.arange(A_BLOCK) // CHUNK
    mask = (cidx[None, :] <= cidx[:, None]).astype(w_s.dtype)
    mixed = jnp.einsum("hij,nbjhc->nbihc", w_s * mask[None], vp)
    mixed = mixed + b_s.T[None, None, :, :, None]
    mixed = mixed.reshape(n, nb * A_BLOCK, A_HEADS, A_HEAD_DIM)[:, :t]
    return u * mixed


def multiscale_pool(xb_ext, pos0):
    n, te, _ = xb_ext.shape
    t = te - POOL_HIST
    xf = xb_ext.astype(jnp.float32)
    csum = jnp.concatenate([jnp.zeros((n, 1, B_WIDTH), jnp.float32), jnp.cumsum(xf, axis=1)], axis=1)
    pos = pos0 + jnp.arange(t)
    outs = []
    for g, w in enumerate(POOL_WINDOWS):
        cg = csum[..., g * B_GROUP_DIM:(g + 1) * B_GROUP_DIM]
        end = cg[:, POOL_HIST + 1:POOL_HIST + 1 + t]
        start = cg[:, POOL_HIST + 1 - w:POOL_HIST + 1 - w + t]
        cnt = jnp.minimum(w, pos + 1).astype(jnp.float32)[None, :, None]
        outs.append((end - start) / cnt)
    return jnp.concatenate(outs, axis=-1) - xf[:, POOL_HIST:]


def even_mixer(h, pos0, hist_pool, w_in, w_out, a_w_s, a_b_s, a_ln_g, a_ln_b, b_pool_w, b_scale):
    n, t, _ = h.shape
    z = h @ w_in
    a = jax.nn.gelu(z[..., :2 * A_WIDTH])
    u = a[..., :A_WIDTH]
    v = layer_norm(a[..., A_WIDTH:], a_ln_g, a_ln_b)
    ya = chunk_gating(u.reshape(n, t, A_HEADS, A_HEAD_DIM), v.reshape(n, t, A_HEADS, A_HEAD_DIM),
                      a_w_s, a_b_s).reshape(n, t, A_WIDTH)
    xb_ext = jnp.concatenate([hist_pool.astype(z.dtype), z[..., 2 * A_WIDTH:]], axis=1)
    pooled = multiscale_pool(xb_ext, pos0).astype(h.dtype).reshape(n, t, B_GROUPS, B_GROUP_DIM)
    yb = jnp.einsum("ntgc,gcd->ntgd", pooled, b_pool_w).reshape(n, t, B_WIDTH) * b_scale
    y = jnp.concatenate([ya, yb], axis=-1) @ w_out
    return y, v, xb_ext[:, -POOL_HIST:]


def odd_mixer(h, hist_c, hist_d, w_in, w_out, c_conv_w, c_conv_b, c_ln_g, c_ln_b, d_conv_w):
    z = h @ w_in
    c_in = z[..., :C_WIDTH] * jax.nn.sigmoid(z[..., C_WIDTH:2 * C_WIDTH])
    c_ext = jnp.concatenate([hist_c.astype(z.dtype), c_in], axis=1)
    c = causal_dwconv(c_ext, c_conv_w) + c_conv_b
    c = jax.nn.silu(layer_norm(c, c_ln_g, c_ln_b))
    o = 2 * C_WIDTH
    gate_b = z[..., o:o + D_WIDTH]
    gate_c = z[..., o + D_WIDTH:o + 2 * D_WIDTH]
    xt = z[..., o + 2 * D_WIDTH:]
    d_ext = jnp.concatenate([hist_d.astype(z.dtype), gate_c * xt], axis=1)
    d = gate_b * causal_dwconv(d_ext, d_conv_w)
    y = jnp.concatenate([c, d], axis=-1) @ w_out
    return y, c_ext[:, -(C_KERNEL - 1):], d_ext[:, -(D_KERNEL - 1):]


def trunk(x, pos0, hist_pool, hist_c, hist_d, norm_mix_g, norm_ffn_g, final_norm_g,
          w_in_even, w_out_even, a_w_s, a_b_s, a_ln_g, a_ln_b, b_pool_w, b_scale,
          w_in_odd, w_out_odd, c_conv_w, c_conv_b, c_ln_g, c_ln_b, d_conv_w, w_ff1, w_ff2):
    vs, pools, cs, ds = [], [], [], []
    for layer in range(DEPTH):
        i = layer // 2
        h = rms_norm(x, norm_mix_g[layer])
        if layer % 2 == 0:
            y, v_rows, pool_rows = even_mixer(h, pos0, hist_pool[i], w_in_even[i], w_out_even[i],
                                              a_w_s[i], a_b_s[i], a_ln_g[i], a_ln_b[i],
                                              b_pool_w[i], b_scale[i])
            vs.append(v_rows)
            pools.append(pool_rows)
        else:
            y, c_rows, d_rows = odd_mixer(h, hist_c[i], hist_d[i], w_in_odd[i], w_out_odd[i],
                                          c_conv_w[i], c_conv_b[i], c_ln_g[i], c_ln_b[i], d_conv_w[i])
            cs.append(c_rows)
            ds.append(d_rows)
        x = x + y
        h = rms_norm(x, norm_ffn_g[layer])
        x = x + jnp.square(jax.nn.relu(h @ w_ff1[layer])) @ w_ff2[layer]
    return rms_norm(x, final_norm_g), jnp.stack(vs), jnp.stack(pools), jnp.stack(cs), jnp.stack(ds)


def setup_inputs(seed: int = 0) -> dict:
    key = jax.random.key(seed)
    ks = jax.random.split(key, 32)
    f32 = jnp.float32

    def nrm(k, shape, scale):
        return jax.random.normal(k, shape, f32) * scale

    return {
        "x_prompt": nrm(ks[0], (BATCH, SEQ, D_MODEL), 1.0),
        "x_sample": nrm(ks[1], (DEC_BATCH, DEC_SEQ, D_MODEL), 1.0),
        "state_pool": nrm(ks[2], (N_EVEN, DEC_BATCH, POOL_HIST, B_WIDTH), 1.0),
        "state_conv_c": nrm(ks[3], (N_ODD, DEC_BATCH, C_KERNEL - 1, C_WIDTH), 0.5),
        "state_conv_d": nrm(ks[4], (N_ODD, DEC_BATCH, D_KERNEL - 1, D_WIDTH), 1.0),
        "norm_mix_g": 1.0 + nrm(ks[5], (DEPTH, D_MODEL), 0.05),
        "norm_ffn_g": 1.0 + nrm(ks[6], (DEPTH, D_MODEL), 0.05),
        "final_norm_g": 1.0 + nrm(ks[7], (D_MODEL,), 0.05),
        "w_in_even": nrm(ks[8], (N_EVEN, D_MODEL, EVEN_IN), D_MODEL ** -0.5),
        "w_out_even": nrm(ks[9], (N_EVEN, MIX_WIDTH, D_MODEL), 0.5 * MIX_WIDTH ** -0.5),
        "a_w_s": nrm(ks[10], (N_EVEN, A_HEADS, A_BLOCK, A_BLOCK), A_BLOCK ** -0.5),
        "a_b_s": 1.0 + nrm(ks[11], (N_EVEN, A_HEADS, A_BLOCK), 0.05),
        "a_ln_g": 1.0 + nrm(ks[12], (N_EVEN, A_WIDTH), 0.05),
        "a_ln_b": nrm(ks[13], (N_EVEN, A_WIDTH), 0.02),
        "b_pool_w": nrm(ks[14], (N_EVEN, B_GROUPS, B_GROUP_DIM, B_GROUP_DIM), B_GROUP_DIM ** -0.5),
        "b_scale": 1.0 + nrm(ks[15], (N_EVEN, B_WIDTH), 0.1),
        "w_in_odd": nrm(ks[16], (N_ODD, D_MODEL, ODD_IN), D_MODEL ** -0.5),
        "w_out_odd": nrm(ks[17], (N_ODD, MIX_WIDTH, D_MODEL), 0.5 * MIX_WIDTH ** -0.5),
        "c_conv_w": nrm(ks[18], (N_ODD, C_KERNEL, C_WIDTH), C_KERNEL ** -0.5),
        "c_conv_b": nrm(ks[19], (N_ODD, C_WIDTH), 0.02),
        "c_ln_g": 1.0 + nrm(ks[20], (N_ODD, C_WIDTH), 0.05),
        "c_ln_b": nrm(ks[21], (N_ODD, C_WIDTH), 0.02),
        "d_conv_w": nrm(ks[22], (N_ODD, D_KERNEL, D_WIDTH), D_KERNEL ** -0.5),
        "w_ff1": nrm(ks[23], (DEPTH, D_MODEL, D_FF), D_MODEL ** -0.5),
        "w_ff2": nrm(ks[24], (DEPTH, D_FF, D_MODEL), 0.5 * D_FF ** -0.5),
    }


def reference(x_prompt, x_sample, state_pool, state_conv_c, state_conv_d,
              norm_mix_g, norm_ffn_g, final_norm_g,
              w_in_even, w_out_even, a_w_s, a_b_s, a_ln_g, a_ln_b, b_pool_w, b_scale,
              w_in_odd, w_out_odd, c_conv_w, c_conv_b, c_ln_g, c_ln_b, d_conv_w, w_ff1, w_ff2):
    nb = x_prompt.shape[0]
    dt = x_prompt.dtype
    zero_pool = jnp.zeros((N_EVEN, nb, POOL_HIST, B_WIDTH), dt)
    zero_c = jnp.zeros((N_ODD, nb, C_KERNEL - 1, C_WIDTH), dt)
    zero_d = jnp.zeros((N_ODD, nb, D_KERNEL - 1, D_WIDTH), dt)
    y_prompt, _, new_pool_p, new_conv_c_p, new_conv_d_p = trunk(
        x_prompt, 0, zero_pool, zero_c, zero_d, norm_mix_g, norm_ffn_g, final_norm_g,
        w_in_even, w_out_even, a_w_s, a_b_s, a_ln_g, a_ln_b, b_pool_w, b_scale,
        w_in_odd, w_out_odd, c_conv_w, c_conv_b, c_ln_g, c_ln_b, d_conv_w, w_ff1, w_ff2)
    y_sample, new_a_v_s, new_pool_s, new_conv_c_s, new_conv_d_s = trunk(
        x_sample, PAST_LEN, state_pool, state_conv_c, state_conv_d, norm_mix_g, norm_ffn_g, final_norm_g,
        w_in_even, w_out_even, a_w_s, a_b_s, a_ln_g, a_ln_b, b_pool_w, b_scale,
        w_in_odd, w_out_odd, c_conv_w, c_conv_b, c_ln_g, c_ln_b, d_conv_w, w_ff1, w_ff2)
    return (y_prompt, y_sample, new_pool_p, new_conv_c_p, new_conv_d_p,
            new_a_v_s, new_pool_s, new_conv_c_s, new_conv_d_s)
```

```python
import functools
from typing import NamedTuple

import jax
import jax.numpy as jnp
from jax import lax
from jax.experimental import pallas as pl
from jax.experimental.pallas import tpu as pltpu

F32 = jnp.float32
BF16 = jnp.bfloat16

EPS = 1e-6
PAST_LEN = 4096
CHUNK = 64
A_BLOCK = 128
A_HEADS = 8
POOL_WINDOWS = (2, 4, 8, 16)
LANES = 128
SUBLANES = 8
VMEM_CAPACITY_BYTES = 64 * 1024 * 1024
ROW_TILE = 512
CONV_ROW_CHUNK = 64
FF_CHUNK = 1024


class Geometry(NamedTuple):
    d_model: int
    tile: int
    prompt_tiles: int
    tiles_per_seq: int
    n_prompt_seq: int
    sample_tiles: int
    sample_len: int
    seqs_per_tile: int

    @property
    def steps(self):
        return self.prompt_tiles + self.sample_tiles


def _round_up(n, m):
    return -(-n // m) * m


def _resident(shape):
    return pl.BlockSpec(shape, lambda i: (0,) * len(shape), pipeline_mode=pl.Buffered(1))


def _vmem_limit(block_bytes, resident_bytes, scratch_bytes, temp_bytes):
    need = 2 * block_bytes + resident_bytes + scratch_bytes + temp_bytes
    assert need <= VMEM_CAPACITY_BYTES, need
    return min(VMEM_CAPACITY_BYTES, _round_up(need + need // 4, 1 << 20))


def _nbytes(shape, dtype):
    n = 1
    for s in shape:
        n *= s
    return n * jnp.dtype(dtype).itemsize


def _rms_norm(x, g):
    y = x * lax.rsqrt(jnp.mean(x * x, axis=-1, keepdims=True) + EPS)
    return y * g


def _layer_norm(x, g, b):
    mu = jnp.mean(x, axis=-1, keepdims=True)
    xc = x - mu
    y = xc * lax.rsqrt(jnp.mean(xc * xc, axis=-1, keepdims=True) + EPS)
    return y * g + b


def _gelu_tanh(x):
    c = 0.7978845608028654
    return x * (0.5 * (1.0 + jnp.tanh(c * (x + 0.044715 * (x * x * x)))))


POOL_HIST = max(POOL_WINDOWS) - 1
POOL_PAD = _round_up(POOL_HIST, SUBLANES)


def _pool_segment(ext, length, pos):
    outs = []
    for g, w in enumerate(POOL_WINDOWS):
        lanes = slice(g * LANES, (g + 1) * LANES)
        cur = ext[POOL_PAD:POOL_PAD + length, lanes]
        s = cur
        for k in range(1, w):
            s = s + ext[POOL_PAD - k:POOL_PAD - k + length, lanes]
        cnt = jnp.minimum(w, pos + 1).astype(F32)
        outs.append(s / cnt - cur)
    return jnp.concatenate(outs, axis=1)


def _even_mixer_kernel(geo, x_ref, g_ref, win_ref, wout_ref, wmix_ref, bmix_ref, lng_ref, lnb_ref,
                       pw_ref, bscale_ref, hist_ref,
                       xo_ref, v_ref, poolp_ref, pools_ref,
                       z_ref, extp_ref, exts_ref, pooled_ref, vb_ref, ycat_ref):
    i = pl.program_id(0)
    tile = geo.tile
    aw = wout_ref.shape[0] // 2
    is_prompt = i < geo.prompt_tiles

    h = _rms_norm(x_ref[...], g_ref[...]).astype(BF16)
    z_ref[...] = jnp.dot(h, win_ref[...], preferred_element_type=F32)

    @pl.when(is_prompt)
    def _():
        j = i % geo.tiles_per_seq

        @pl.when(j == 0)
        def _():
            extp_ref[0:POOL_PAD, :] = jnp.zeros((POOL_PAD, extp_ref.shape[1]), F32)

        extp_ref[POOL_PAD:POOL_PAD + tile, :] = z_ref[:, 2 * aw:]
        pos = j * tile + lax.broadcasted_iota(jnp.int32, (tile, 1), 0)
        pooled_ref[...] = _pool_segment(extp_ref, tile, pos)

        @pl.when(j == geo.tiles_per_seq - 1)
        def _():
            poolp_ref[0] = extp_ref[POOL_PAD + tile - POOL_HIST:POOL_PAD + tile, :]

        extp_ref[0:POOL_PAD, :] = extp_ref[tile:tile + POOL_PAD, :]

    @pl.when(jnp.logical_not(is_prompt))
    def _():
        n = geo.sample_len
        pos = PAST_LEN + lax.broadcasted_iota(jnp.int32, (n, 1), 0)
        for s in range(geo.seqs_per_tile):
            ext = exts_ref.at[s]
            ext[POOL_PAD - POOL_HIST:POOL_PAD, :] = hist_ref[s]
            ext[POOL_PAD:POOL_PAD + n, :] = z_ref[s * n:(s + 1) * n, 2 * aw:]
            pooled_ref[s * n:(s + 1) * n, :] = _pool_segment(ext, n, pos)
            pools_ref[s] = ext[POOL_PAD + n - POOL_HIST:POOL_PAD + n, :]

    lane = lax.broadcasted_iota(jnp.int32, (A_BLOCK, LANES), 1)
    low_head = lane < (LANES // 2)
    n_pairs = aw // LANES
    for b in range(tile // A_BLOCK):
        rows = slice(b * A_BLOCK, (b + 1) * A_BLOCK)
        v = _layer_norm(_gelu_tanh(z_ref[rows, aw:2 * aw]), lng_ref[...], lnb_ref[...])
        vb_ref[rows, :] = v

        mixed = []
        for p in range(n_pairs):
            vp = v[:, p * LANES:(p + 1) * LANES].astype(BF16)
            zero = jnp.zeros_like(vp)
            rhs = jnp.concatenate([jnp.where(low_head, vp, zero), jnp.where(low_head, zero, vp)], axis=0)
            mixed.append(jnp.dot(wmix_ref[0, p], rhs, preferred_element_type=F32))
        mixed = jnp.concatenate(mixed, axis=1) + bmix_ref[0]
        u = _gelu_tanh(z_ref[rows, 0:aw])
        ycat_ref[rows, 0:aw] = (u * mixed).astype(BF16)

    @pl.when(jnp.logical_not(is_prompt))
    def _():
        v_ref[...] = vb_ref[...]

    for g in range(len(POOL_WINDOWS)):
        lanes = slice(g * LANES, (g + 1) * LANES)
        yb = jnp.dot(pooled_ref[:, lanes].astype(BF16), pw_ref[g], preferred_element_type=F32)
        ycat_ref[:, aw + g * LANES:aw + (g + 1) * LANES] = (yb * bscale_ref[:, lanes]).astype(BF16)

    y = jnp.dot(ycat_ref[...], wout_ref[...], preferred_element_type=F32)
    xo_ref[...] = x_ref[...] + y


def _even_mixer(geo, layer_idx, x, g, w_in, w_out, wmix, bmix, ln_g, ln_b, pool_w, b_scale, state_pool):
    n, d = x.shape
    tile = geo.tile
    in_w = w_in.shape[1]
    aw = w_out.shape[0] // 2
    bw = in_w - 2 * aw
    pt = geo.prompt_tiles
    n_sample_seq = state_pool.shape[1]
    spt = geo.seqs_per_tile

    def sample_tile(i):
        return jnp.maximum(i - pt, 0)

    in_specs = [
        pl.BlockSpec((tile, d), lambda i: (i, 0)),
        _resident((1, d)),
        _resident(w_in.shape),
        _resident(w_out.shape),
        pl.BlockSpec((1,) + wmix.shape[1:], lambda i: (i // pt, 0, 0, 0)),
        pl.BlockSpec((1,) + bmix.shape[1:], lambda i: (i // pt, 0, 0)),
        _resident((1, aw)),
        _resident((1, aw)),
        _resident(pool_w.shape),
        _resident((1, bw)),
        pl.BlockSpec((None, spt, POOL_HIST, bw), lambda i: (layer_idx, sample_tile(i), 0, 0)),
    ]
    out_shape = [
        jax.ShapeDtypeStruct((n, d), F32),
        jax.ShapeDtypeStruct((geo.sample_tiles * tile, aw), F32),
        jax.ShapeDtypeStruct((geo.n_prompt_seq, POOL_HIST, bw), F32),
        jax.ShapeDtypeStruct((n_sample_seq, POOL_HIST, bw), F32),
    ]
    out_specs = [
        pl.BlockSpec((tile, d), lambda i: (i, 0)),
        pl.BlockSpec((tile, aw), lambda i: (sample_tile(i), 0)),
        pl.BlockSpec((1, POOL_HIST, bw),
                     lambda i: (jnp.minimum(i // geo.tiles_per_seq, geo.n_prompt_seq - 1), 0, 0)),
        pl.BlockSpec((spt, POOL_HIST, bw), lambda i: (sample_tile(i), 0, 0)),
    ]
    scratch = [
        ((tile, in_w), F32),
        ((POOL_PAD + tile, bw), F32),
        ((spt, POOL_PAD + geo.sample_len, bw), F32),
        ((tile, bw), F32),
        ((tile, aw), F32),
        ((tile, aw + bw), BF16),
    ]
    block_bytes = (2 * _nbytes((tile, d), F32) + _nbytes((tile, aw), F32)
                   + _nbytes(wmix.shape[1:], BF16) + _nbytes(bmix.shape[1:], F32)
                   + 3 * _nbytes((spt, POOL_PAD, bw), F32))
    resident_bytes = _nbytes(w_in.shape, BF16) + _nbytes(w_out.shape, BF16) + _nbytes(pool_w.shape, BF16)
    scratch_bytes = sum(_nbytes(s, t) for s, t in scratch)
    temp_bytes = _nbytes((tile, in_w), F32) + _nbytes((tile, d), F32)

    return pl.pallas_call(
        functools.partial(_even_mixer_kernel, geo),
        grid=(geo.steps,),
        in_specs=in_specs,
        out_specs=out_specs,
        out_shape=out_shape,
        scratch_shapes=[pltpu.VMEM(s, t) for s, t in scratch],
        compiler_params=pltpu.CompilerParams(
            dimension_semantics=("arbitrary",),
            vmem_limit_bytes=_vmem_limit(block_bytes, resident_bytes, scratch_bytes, temp_bytes)),
        name=f"even_mixer_{layer_idx}",
    )(x, g, w_in, w_out, wmix, bmix, ln_g, ln_b, pool_w, b_scale, state_pool)


def _dwconv_rows(ext, first_row, w_ref, n_taps, rows, lanes):
    acc = None
    for k in range(n_taps):
        term = ext[first_row + k + rows.start:first_row + k + rows.stop, lanes] * w_ref[k:k + 1, lanes]
        acc = term if acc is None else acc + term
    return acc


def _odd_mixer_kernel(geo, x_ref, g_ref, win_ref, wout_ref, cw_ref, cb_ref, lng_ref, lnb_ref, dw_ref,
                      histc_ref, histd_ref,
                      xo_ref, cp_ref, dp_ref, cs_ref, ds_ref,
                      z_ref, extc_ref, extd_ref, extcs_ref, extds_ref, conv_ref, ycat_ref):
    i = pl.program_id(0)
    tile = geo.tile
    cwid = wout_ref.shape[0] // 2
    kc = cw_ref.shape[0]
    kd = dw_ref.shape[0]
    hc, hd = kc - 1, kd - 1
    pc, pd = _round_up(hc, SUBLANES), _round_up(hd, SUBLANES)
    is_prompt = i < geo.prompt_tiles
    o = 2 * cwid

    h = _rms_norm(x_ref[...], g_ref[...]).astype(BF16)
    z_ref[...] = jnp.dot(h, win_ref[...], preferred_element_type=F32)

    def glu(rows):
        return z_ref[rows, 0:cwid] * jax.nn.sigmoid(z_ref[rows, cwid:2 * cwid])

    def gated(rows):
        return z_ref[rows, o + cwid:o + 2 * cwid] * z_ref[rows, o + 2 * cwid:o + 3 * cwid]

    def convolve(extc, extd, base, length):
        chunk = min(CONV_ROW_CHUNK, length)
        for lt in range(cwid // LANES):
            lanes = slice(lt * LANES, (lt + 1) * LANES)
            for r0 in range(0, length, chunk):
                rows = slice(r0, r0 + chunk)
                out_rows = slice(base + r0, base + r0 + chunk)
                conv_ref[out_rows, lanes] = (_dwconv_rows(extc, pc - hc, cw_ref, kc, rows, lanes)
                                             + cb_ref[:, lanes])
                d = _dwconv_rows(extd, pd - hd, dw_ref, kd, rows, lanes)
                ycat_ref[out_rows, cwid + lt * LANES:cwid + (lt + 1) * LANES] = (
                    z_ref[out_rows, o + lt * LANES:o + (lt + 1) * LANES] * d).astype(BF16)

    @pl.when(is_prompt)
    def _():
        j = i % geo.tiles_per_seq

        @pl.when(j == 0)
        def _():
            extc_ref[0:pc, :] = jnp.zeros((pc, cwid), F32)
            extd_ref[0:pd, :] = jnp.zeros((pd, cwid), F32)

        all_rows = slice(0, tile)
        extc_ref[pc:pc + tile, :] = glu(all_rows)
        extd_ref[pd:pd + tile, :] = gated(all_rows)
        convolve(extc_ref, extd_ref, 0, tile)

        @pl.when(j == geo.tiles_per_seq - 1)
        def _():
            cp_ref[0] = extc_ref[pc + tile - hc:pc + tile, :]
            dp_ref[0] = extd_ref[pd + tile - hd:pd + tile, :]

        extc_ref[0:pc, :] = extc_ref[tile:tile + pc, :]
        extd_ref[0:pd, :] = extd_ref[tile:tile + pd, :]

    @pl.when(jnp.logical_not(is_prompt))
    def _():
        n = geo.sample_len
        for s in range(geo.seqs_per_tile):
            rows = slice(s * n, (s + 1) * n)
            extc = extcs_ref.at[s]
            extd = extds_ref.at[s]
            extc[pc - hc:pc, :] = histc_ref[s]
            extd[pd - hd:pd, :] = histd_ref[s]
            extc[pc:pc + n, :] = glu(rows)
            extd[pd:pd + n, :] = gated(rows)
            convolve(extc, extd, s * n, n)
            cs_ref[s] = extc[pc + n - hc:pc + n, :]
            ds_ref[s] = extd[pd + n - hd:pd + n, :]

    for b in range(tile // A_BLOCK):
        rows = slice(b * A_BLOCK, (b + 1) * A_BLOCK)
        c = _layer_norm(conv_ref[rows, :], lng_ref[...], lnb_ref[...])
        ycat_ref[rows, 0:cwid] = (c * jax.nn.sigmoid(c)).astype(BF16)

    y = jnp.dot(ycat_ref[...], wout_ref[...], preferred_element_type=F32)
    xo_ref[...] = x_ref[...] + y


def _odd_mixer(geo, layer_idx, x, g, w_in, w_out, conv_w, conv_b, ln_g, ln_b, dconv_w, state_c, state_d):
    n, d = x.shape
    tile = geo.tile
    in_w = w_in.shape[1]
    cwid = w_out.shape[0] // 2
    kc, kd = conv_w.shape[0], dconv_w.shape[0]
    hc, hd = kc - 1, kd - 1
    pc, pd = _round_up(hc, SUBLANES), _round_up(hd, SUBLANES)
    pt = geo.prompt_tiles
    n_sample_seq = state_c.shape[1]
    spt = geo.seqs_per_tile

    def sample_tile(i):
        return jnp.maximum(i - pt, 0)

    def prompt_seq(i):
        return jnp.minimum(i // geo.tiles_per_seq, geo.n_prompt_seq - 1)

    in_specs = [
        pl.BlockSpec((tile, d), lambda i: (i, 0)),
        _resident((1, d)),
        _resident(w_in.shape),
        _resident(w_out.shape),
        _resident(conv_w.shape),
        _resident((1, cwid)),
        _resident((1, cwid)),
        _resident((1, cwid)),
        _resident(dconv_w.shape),
        pl.BlockSpec((None, spt, hc, cwid), lambda i: (layer_idx, sample_tile(i), 0, 0)),
        pl.BlockSpec((None, spt, hd, cwid), lambda i: (layer_idx, sample_tile(i), 0, 0)),
    ]
    out_shape = [
        jax.ShapeDtypeStruct((n, d), F32),
        jax.ShapeDtypeStruct((geo.n_prompt_seq, hc, cwid), F32),
        jax.ShapeDtypeStruct((geo.n_prompt_seq, hd, cwid), F32),
        jax.ShapeDtypeStruct((n_sample_seq, hc, cwid), F32),
        jax.ShapeDtypeStruct((n_sample_seq, hd, cwid), F32),
    ]
    out_specs = [
        pl.BlockSpec((tile, d), lambda i: (i, 0)),
        pl.BlockSpec((1, hc, cwid), lambda i: (prompt_seq(i), 0, 0)),
        pl.BlockSpec((1, hd, cwid), lambda i: (prompt_seq(i), 0, 0)),
        pl.BlockSpec((spt, hc, cwid), lambda i: (sample_tile(i), 0, 0)),
        pl.BlockSpec((spt, hd, cwid), lambda i: (sample_tile(i), 0, 0)),
    ]
    scratch = [
        ((tile, in_w), F32),
        ((pc + tile, cwid), F32),
        ((pd + tile, cwid), F32),
        ((spt, pc + geo.sample_len, cwid), F32),
        ((spt, pd + geo.sample_len, cwid), F32),
        ((tile, cwid), F32),
        ((tile, 2 * cwid), BF16),
    ]
    block_bytes = (2 * _nbytes((tile, d), F32)
                   + 3 * _nbytes((spt, pc, cwid), F32) + 3 * _nbytes((spt, pd, cwid), F32))
    resident_bytes = _nbytes(w_in.shape, BF16) + _nbytes(w_out.shape, BF16)
    scratch_bytes = sum(_nbytes(s, t) for s, t in scratch)
    temp_bytes = _nbytes((tile, in_w), F32) + _nbytes((tile, d), F32)

    return pl.pallas_call(
        functools.partial(_odd_mixer_kernel, geo),
        grid=(geo.steps,),
        in_specs=in_specs,
        out_specs=out_specs,
        out_shape=out_shape,
        scratch_shapes=[pltpu.VMEM(s, t) for s, t in scratch],
        compiler_params=pltpu.CompilerParams(
            dimension_semantics=("arbitrary",),
            vmem_limit_bytes=_vmem_limit(block_bytes, resident_bytes, scratch_bytes, temp_bytes)),
        name=f"odd_mixer_{layer_idx}",
    )(x, g, w_in, w_out, conv_w, conv_b, ln_g, ln_b, dconv_w, state_c, state_d)


def _ffn_kernel(x_ref, g_ref, w1_ref, w2_ref, gf_ref, xo_ref, *, final_norm):
    x = x_ref[...]
    h = _rms_norm(x, g_ref[...]).astype(BF16)
    acc = x
    for c0 in range(0, w1_ref.shape[1], FF_CHUNK):
        t = jnp.maximum(jnp.dot(h, w1_ref[:, c0:c0 + FF_CHUNK], preferred_element_type=F32), 0.0)
        acc = acc + jnp.dot((t * t).astype(BF16), w2_ref[c0:c0 + FF_CHUNK, :], preferred_element_type=F32)
    xo_ref[...] = _rms_norm(acc, gf_ref[...]) if final_norm else acc


def _ffn(geo, layer, x, g, w1, w2, g_final, final_norm):
    n, d = x.shape
    tile = geo.tile
    block_bytes = 2 * _nbytes((tile, d), F32)
    resident_bytes = _nbytes(w1.shape, BF16) + _nbytes(w2.shape, BF16)
    temp_bytes = 2 * _nbytes((tile, FF_CHUNK), F32) + 2 * _nbytes((tile, d), F32)
    return pl.pallas_call(
        functools.partial(_ffn_kernel, final_norm=final_norm),
        grid=(n // tile,),
        in_specs=[pl.BlockSpec((tile, d), lambda i: (i, 0)), _resident((1, d)),
                  _resident(w1.shape), _resident(w2.shape), _resident((1, d))],
        out_specs=pl.BlockSpec((tile, d), lambda i: (i, 0)),
        out_shape=jax.ShapeDtypeStruct((n, d), F32),
        compiler_params=pltpu.CompilerParams(
            dimension_semantics=("arbitrary",),
            vmem_limit_bytes=_vmem_limit(block_bytes, resident_bytes, 0, temp_bytes)),
        name=f"ffn_{layer}",
    )(x, g, w1, w2, g_final)


def _gating_operands(w_s, b_s, sample_len):
    heads, blk, _ = w_s.shape
    cidx = jnp.arange(blk) // CHUNK
    mask = (cidx[None, :] <= cidx[:, None]).astype(w_s.dtype)
    w_prompt = w_s * mask[None]
    head_dim_lanes = LANES // 2
    reps = blk // sample_len
    eye = jnp.eye(reps, dtype=w_s.dtype)
    w_first = w_prompt[:, :sample_len, :sample_len]
    w_sample = jnp.einsum("ab,hij->haibj", eye, w_first).reshape(heads, blk, blk)
    b_sample = jnp.tile(b_s[:, :sample_len], (1, reps))

    def pairs(w):
        return jnp.concatenate([w[0::2], w[1::2]], axis=2)

    def bias(b):
        return jnp.repeat(b.T, head_dim_lanes, axis=1)

    wmix = jnp.stack([pairs(w_prompt), pairs(w_sample)]).astype(BF16)
    bmix = jnp.stack([bias(b_s), bias(b_sample)]).astype(F32)
    return wmix, bmix


def kernel(x_prompt, x_sample, state_pool, state_conv_c, state_conv_d, norm_mix_g, norm_ffn_g, final_norm_g,
           w_in_even, w_out_even, a_w_s, a_b_s, a_ln_g, a_ln_b, b_pool_w, b_scale,
           w_in_odd, w_out_odd, c_conv_w, c_conv_b, c_ln_g, c_ln_b, d_conv_w, w_ff1, w_ff2):
    n_prompt_seq, seq, d = x_prompt.shape
    n_sample_seq, sample_len, _ = x_sample.shape
    depth = norm_mix_g.shape[0]
    tile = ROW_TILE
    assert seq % tile == 0 and tile % A_BLOCK == 0 and tile % sample_len == 0
    assert A_BLOCK % sample_len == 0 and sample_len <= CHUNK
    assert (n_sample_seq * sample_len) % tile == 0
    assert a_w_s.shape[1] == A_HEADS and a_w_s.shape[2] == A_BLOCK
    assert w_out_even.shape[1] // 2 == A_HEADS * (LANES // 2)
    geo = Geometry(
        d_model=d, tile=tile,
        prompt_tiles=n_prompt_seq * seq // tile, tiles_per_seq=seq // tile, n_prompt_seq=n_prompt_seq,
        sample_tiles=n_sample_seq * sample_len // tile, sample_len=sample_len,
        seqs_per_tile=tile // sample_len)

    x = jnp.concatenate([x_prompt.reshape(-1, d), x_sample.reshape(-1, d)], axis=0)
    n_prompt_rows = n_prompt_seq * seq

    def row(v):
        return v.reshape(1, -1)

    pools_p, pools_s, vs, cs_p, cs_s, ds_p, ds_s = [], [], [], [], [], [], []
    for layer in range(depth):
        li = layer // 2
        if layer % 2 == 0:
            wmix, bmix = _gating_operands(a_w_s[li], a_b_s[li], sample_len)
            x, v_s, pool_p, pool_s = _even_mixer(
                geo, li, x, row(norm_mix_g[layer]), w_in_even[li].astype(BF16), w_out_even[li].astype(BF16),
                wmix, bmix, row(a_ln_g[li]), row(a_ln_b[li]), b_pool_w[li].astype(BF16), row(b_scale[li]),
                state_pool)
            vs.append(v_s.reshape(n_sample_seq, sample_len, -1))
            pools_p.append(pool_p)
            pools_s.append(pool_s)
        else:
            x, c_p, d_p, c_s, d_s = _odd_mixer(
                geo, li, x, row(norm_mix_g[layer]), w_in_odd[li].astype(BF16), w_out_odd[li].astype(BF16),
                c_conv_w[li], row(c_conv_b[li]), row(c_ln_g[li]), row(c_ln_b[li]), d_conv_w[li],
                state_conv_c, state_conv_d)
            cs_p.append(c_p)
            ds_p.append(d_p)
            cs_s.append(c_s)
            ds_s.append(d_s)
        x = _ffn(geo, layer, x, row(norm_ffn_g[layer]), w_ff1[layer].astype(BF16), w_ff2[layer].astype(BF16),
                 row(final_norm_g), final_norm=(layer == depth - 1))

    y_prompt = x[:n_prompt_rows].reshape(n_prompt_seq, seq, d)
    y_sample = x[n_prompt_rows:].reshape(n_sample_seq, sample_len, d)
    return (y_prompt, y_sample, jnp.stack(pools_p), jnp.stack(cs_p), jnp.stack(ds_p),
            jnp.stack(vs), jnp.stack(pools_s), jnp.stack(cs_s), jnp.stack(ds_s))
```

```python
import functools
from typing import NamedTuple

import jax
import jax.numpy as jnp
from jax import lax
from jax.experimental import pallas as pl
from jax.experimental.pallas import tpu as pltpu

F32 = jnp.float32
BF16 = jnp.bfloat16

EPS = 1e-6
PAST_LEN = 4096
CHUNK = 64
A_BLOCK = 128
A_HEADS = 8
POOL_WINDOWS = (2, 4, 8, 16)
LANES = 128
SUBLANES = 8
VMEM_CAPACITY_BYTES = 64 * 1024 * 1024
ROW_TILE = 512
CONV_ROW_CHUNK = 64
POOL_ROW_CHUNK = 64
FF_CHUNK = 1024


class Geometry(NamedTuple):
    d_model: int
    tile: int
    prompt_tiles: int
    tiles_per_seq: int
    n_prompt_seq: int
    sample_tiles: int
    sample_len: int
    seqs_per_tile: int

    @property
    def steps(self):
        return self.prompt_tiles + self.sample_tiles


def _round_up(n, m):
    return -(-n // m) * m


def _resident(shape):
    return pl.BlockSpec(shape, lambda i: (0,) * len(shape), pipeline_mode=pl.Buffered(1))


def _vmem_limit(block_bytes, resident_bytes, scratch_bytes, temp_bytes):
    need = 2 * block_bytes + resident_bytes + scratch_bytes + temp_bytes
    assert need <= VMEM_CAPACITY_BYTES, need
    return min(VMEM_CAPACITY_BYTES, _round_up(need + need // 4, 1 << 20))


def _nbytes(shape, dtype):
    n = 1
    for s in shape:
        n *= s
    return n * jnp.dtype(dtype).itemsize


def _rms_norm(x, g):
    y = x * lax.rsqrt(jnp.mean(x * x, axis=-1, keepdims=True) + EPS)
    return y * g


def _layer_norm(x, g, b):
    mu = jnp.mean(x, axis=-1, keepdims=True)
    xc = x - mu
    y = xc * lax.rsqrt(jnp.mean(xc * xc, axis=-1, keepdims=True) + EPS)
    return y * g + b


def _gelu_tanh(x):
    c = 0.7978845608028654
    return x * (0.5 * (1.0 + jnp.tanh(c * (x + 0.044715 * (x * x * x)))))


POOL_HIST = max(POOL_WINDOWS) - 1
POOL_PAD = _round_up(POOL_HIST, SUBLANES)


def _pool_segment(ext, length, pos0, out_ref, out_base):
    chunk = min(POOL_ROW_CHUNK, length)
    for r0 in range(0, length, chunk):
        pos = pos0 + r0 + lax.broadcasted_iota(jnp.int32, (chunk, 1), 0)
        for g, w in enumerate(POOL_WINDOWS):
            lanes = slice(g * LANES, (g + 1) * LANES)
            win = ext[r0:r0 + POOL_PAD + chunk, lanes]
            s = win
            span = 1
            while span < w:
                s = s + pltpu.roll(s, span, axis=0)
                span *= 2
            inv_cnt = 1.0 / jnp.minimum(w, pos + 1).astype(F32)
            out_ref[out_base + r0:out_base + r0 + chunk, lanes] = s[POOL_PAD:] * inv_cnt - win[POOL_PAD:]


def _load_rows(x_refs, is_prompt):
    if len(x_refs) == 1:
        return x_refs[0][...]
    return jnp.where(is_prompt, x_refs[0][...], x_refs[1][...])


def _even_mixer_kernel(geo, n_x, *refs):
    x_refs = refs[:n_x]
    (g_ref, win_ref, wout_ref, wmix_ref, bmix_ref, lng_ref, lnb_ref, pw_ref, bscale_ref, hist_ref,
     xo_ref, v_ref, poolp_ref, pools_ref,
     z_ref, extp_ref, exts_ref, pooled_ref, vb_ref, ycat_ref) = refs[n_x:]
    i = pl.program_id(0)
    tile = geo.tile
    aw = wout_ref.shape[0] // 2
    is_prompt = i < geo.prompt_tiles

    h = _rms_norm(_load_rows(x_refs, is_prompt), g_ref[...]).astype(BF16)
    z_ref[...] = jnp.dot(h, win_ref[...], preferred_element_type=F32)

    @pl.when(is_prompt)
    def _():
        j = i % geo.tiles_per_seq

        @pl.when(j == 0)
        def _():
            extp_ref[0:POOL_PAD, :] = jnp.zeros((POOL_PAD, extp_ref.shape[1]), F32)

        extp_ref[POOL_PAD:POOL_PAD + tile, :] = z_ref[:, 2 * aw:]
        _pool_segment(extp_ref, tile, j * tile, pooled_ref, 0)

        @pl.when(j == geo.tiles_per_seq - 1)
        def _():
            poolp_ref[0] = extp_ref[POOL_PAD + tile - POOL_HIST:POOL_PAD + tile, :]

        extp_ref[0:POOL_PAD, :] = extp_ref[tile:tile + POOL_PAD, :]

    @pl.when(jnp.logical_not(is_prompt))
    def _():
        n = geo.sample_len
        for s in range(geo.seqs_per_tile):
            ext = exts_ref.at[s]
            ext[0:POOL_PAD - POOL_HIST, :] = jnp.zeros((POOL_PAD - POOL_HIST, ext.shape[1]), F32)
            ext[POOL_PAD - POOL_HIST:POOL_PAD, :] = hist_ref[s]
            ext[POOL_PAD:POOL_PAD + n, :] = z_ref[s * n:(s + 1) * n, 2 * aw:]
            _pool_segment(ext, n, PAST_LEN, pooled_ref, s * n)
            pools_ref[s] = ext[POOL_PAD + n - POOL_HIST:POOL_PAD + n, :]

    lane = lax.broadcasted_iota(jnp.int32, (A_BLOCK, LANES), 1)
    low_head = lane < (LANES // 2)
    n_pairs = aw // LANES
    for b in range(tile // A_BLOCK):
        rows = slice(b * A_BLOCK, (b + 1) * A_BLOCK)
        v = _layer_norm(_gelu_tanh(z_ref[rows, aw:2 * aw]), lng_ref[...], lnb_ref[...])
        vb_ref[rows, :] = v

        mixed = []
        for p in range(n_pairs):
            vp = v[:, p * LANES:(p + 1) * LANES].astype(BF16)
            zero = jnp.zeros_like(vp)
            rhs = jnp.concatenate([jnp.where(low_head, vp, zero), jnp.where(low_head, zero, vp)], axis=0)
            mixed.append(jnp.dot(wmix_ref[0, p], rhs, preferred_element_type=F32))
        mixed = jnp.concatenate(mixed, axis=1) + bmix_ref[0]
        u = _gelu_tanh(z_ref[rows, 0:aw])
        ycat_ref[rows, 0:aw] = (u * mixed).astype(BF16)

    @pl.when(jnp.logical_not(is_prompt))
    def _():
        v_ref[...] = vb_ref[...]

    for g in range(len(POOL_WINDOWS)):
        lanes = slice(g * LANES, (g + 1) * LANES)
        yb = jnp.dot(pooled_ref[:, lanes].astype(BF16), pw_ref[g], preferred_element_type=F32)
        ycat_ref[:, aw + g * LANES:aw + (g + 1) * LANES] = (yb * bscale_ref[:, lanes]).astype(BF16)

    y = jnp.dot(ycat_ref[...], wout_ref[...], preferred_element_type=F32)
    xo_ref[...] = _load_rows(x_refs, is_prompt) + y


def _even_mixer(geo, layer_idx, xs, g, w_in, w_out, wmix, bmix, ln_g, ln_b, pool_w, b_scale, state_pool):
    d = xs[0].shape[1]
    n = sum(x.shape[0] for x in xs)
    tile = geo.tile
    in_w = w_in.shape[1]
    aw = w_out.shape[0] // 2
    bw = in_w - 2 * aw
    pt = geo.prompt_tiles
    n_sample_seq = state_pool.shape[1]
    spt = geo.seqs_per_tile

    def sample_tile(i):
        return jnp.maximum(i - pt, 0)

    if len(xs) == 1:
        x_specs = [pl.BlockSpec((tile, d), lambda i: (i, 0))]
    else:
        x_specs = [pl.BlockSpec((tile, d), lambda i: (jnp.minimum(i, pt - 1), 0)),
                   pl.BlockSpec((tile, d), lambda i: (sample_tile(i), 0))]
    in_specs = x_specs + [
        _resident((1, d)),
        _resident(w_in.shape),
        _resident(w_out.shape),
        pl.BlockSpec((1,) + wmix.shape[1:], lambda i: (i // pt, 0, 0, 0)),
        pl.BlockSpec((1,) + bmix.shape[1:], lambda i: (i // pt, 0, 0)),
        _resident((1, aw)),
        _resident((1, aw)),
        _resident(pool_w.shape),
        _resident((1, bw)),
        pl.BlockSpec((None, spt, POOL_HIST, bw), lambda i: (layer_idx, sample_tile(i), 0, 0)),
    ]
    out_shape = [
        jax.ShapeDtypeStruct((n, d), F32),
        jax.ShapeDtypeStruct((geo.sample_tiles * tile, aw), F32),
        jax.ShapeDtypeStruct((geo.n_prompt_seq, POOL_HIST, bw), F32),
        jax.ShapeDtypeStruct((n_sample_seq, POOL_HIST, bw), F32),
    ]
    out_specs = [
        pl.BlockSpec((tile, d), lambda i: (i, 0)),
        pl.BlockSpec((tile, aw), lambda i: (sample_tile(i), 0)),
        pl.BlockSpec((1, POOL_HIST, bw),
                     lambda i: (jnp.minimum(i // geo.tiles_per_seq, geo.n_prompt_seq - 1), 0, 0)),
        pl.BlockSpec((spt, POOL_HIST, bw), lambda i: (sample_tile(i), 0, 0)),
    ]
    scratch = [
        ((tile, in_w), F32),
        ((POOL_PAD + tile, bw), F32),
        ((spt, POOL_PAD + geo.sample_len, bw), F32),
        ((tile, bw), F32),
        ((tile, aw), F32),
        ((tile, aw + bw), BF16),
    ]
    block_bytes = ((1 + len(xs)) * _nbytes((tile, d), F32) + _nbytes((tile, aw), F32)
                   + _nbytes(wmix.shape[1:], BF16) + _nbytes(bmix.shape[1:], F32)
                   + 3 * _nbytes((spt, POOL_PAD, bw), F32))
    resident_bytes = _nbytes(w_in.shape, BF16) + _nbytes(w_out.shape, BF16) + _nbytes(pool_w.shape, BF16)
    scratch_bytes = sum(_nbytes(s, t) for s, t in scratch)
    temp_bytes = _nbytes((tile, in_w), F32) + _nbytes((tile, d), F32)

    return pl.pallas_call(
        functools.partial(_even_mixer_kernel, geo, len(xs)),
        grid=(geo.steps,),
        in_specs=in_specs,
        out_specs=out_specs,
        out_shape=out_shape,
        scratch_shapes=[pltpu.VMEM(s, t) for s, t in scratch],
        compiler_params=pltpu.CompilerParams(
            dimension_semantics=("arbitrary",),
            vmem_limit_bytes=_vmem_limit(block_bytes, resident_bytes, scratch_bytes, temp_bytes)),
        name=f"even_mixer_{layer_idx}",
    )(*xs, g, w_in, w_out, wmix, bmix, ln_g, ln_b, pool_w, b_scale, state_pool)


def _dwconv_chunk(ext, first_row, w_ref, n_taps, r0, chunk, lanes):
    wlen = chunk + _round_up(first_row + n_taps - 1, SUBLANES)
    win = ext[r0:r0 + wlen, lanes]
    acc = None
    for r in range(SUBLANES):
        taps = [k for k in range(n_taps) if (first_row + k) % SUBLANES == r]
        if not taps:
            continue
        shifted = win if r == 0 else pltpu.roll(win, wlen - r, axis=0)
        for k in taps:
            q0 = (first_row + k) // SUBLANES * SUBLANES
            term = shifted[q0:q0 + chunk] * w_ref[k:k + 1, lanes]
            acc = term if acc is None else acc + term
    return acc


def _odd_mixer_kernel(geo, x_ref, g_ref, win_ref, wout_ref, cw_ref, cb_ref, lng_ref, lnb_ref, dw_ref,
                      histc_ref, histd_ref,
                      xo_ref, cp_ref, dp_ref, cs_ref, ds_ref,
                      z_ref, extc_ref, extd_ref, extcs_ref, extds_ref, conv_ref, ycat_ref):
    i = pl.program_id(0)
    tile = geo.tile
    cwid = wout_ref.shape[0] // 2
    kc = cw_ref.shape[0]
    kd = dw_ref.shape[0]
    hc, hd = kc - 1, kd - 1
    pc, pd = _round_up(hc, SUBLANES), _round_up(hd, SUBLANES)
    is_prompt = i < geo.prompt_tiles
    o = 2 * cwid

    h = _rms_norm(x_ref[...], g_ref[...]).astype(BF16)
    z_ref[...] = jnp.dot(h, win_ref[...], preferred_element_type=F32)

    def glu(rows):
        return z_ref[rows, 0:cwid] * jax.nn.sigmoid(z_ref[rows, cwid:2 * cwid])

    def gated(rows):
        return z_ref[rows, o + cwid:o + 2 * cwid] * z_ref[rows, o + 2 * cwid:o + 3 * cwid]

    def convolve(extc, extd, base, length):
        chunk = min(CONV_ROW_CHUNK, length)
        for lt in range(cwid // LANES):
            lanes = slice(lt * LANES, (lt + 1) * LANES)
            for r0 in range(0, length, chunk):
                out_rows = slice(base + r0, base + r0 + chunk)
                conv_ref[out_rows, lanes] = (_dwconv_chunk(extc, pc - hc, cw_ref, kc, r0, chunk, lanes)
                                             + cb_ref[:, lanes])
                d = _dwconv_chunk(extd, pd - hd, dw_ref, kd, r0, chunk, lanes)
                ycat_ref[out_rows, cwid + lt * LANES:cwid + (lt + 1) * LANES] = (
                    z_ref[out_rows, o + lt * LANES:o + (lt + 1) * LANES] * d).astype(BF16)

    @pl.when(is_prompt)
    def _():
        j = i % geo.tiles_per_seq

        @pl.when(j == 0)
        def _():
            extc_ref[0:pc, :] = jnp.zeros((pc, cwid), F32)
            extd_ref[0:pd, :] = jnp.zeros((pd, cwid), F32)

        all_rows = slice(0, tile)
        extc_ref[pc:pc + tile, :] = glu(all_rows)
        extd_ref[pd:pd + tile, :] = gated(all_rows)
        convolve(extc_ref, extd_ref, 0, tile)

        @pl.when(j == geo.tiles_per_seq - 1)
        def _():
            cp_ref[0] = extc_ref[pc + tile - hc:pc + tile, :]
            dp_ref[0] = extd_ref[pd + tile - hd:pd + tile, :]

        extc_ref[0:pc, :] = extc_ref[tile:tile + pc, :]
        extd_ref[0:pd, :] = extd_ref[tile:tile + pd, :]

    @pl.when(jnp.logical_not(is_prompt))
    def _():
        n = geo.sample_len
        for s in range(geo.seqs_per_tile):
            rows = slice(s * n, (s + 1) * n)
            extc = extcs_ref.at[s]
            extd = extds_ref.at[s]
            extc[0:pc - hc, :] = jnp.zeros((pc - hc, cwid), F32)
            extd[0:pd - hd, :] = jnp.zeros((pd - hd, cwid), F32)
            extc[pc - hc:pc, :] = histc_ref[s]
            extd[pd - hd:pd, :] = histd_ref[s]
            extc[pc:pc + n, :] = glu(rows)
            extd[pd:pd + n, :] = gated(rows)
            convolve(extc, extd, s * n, n)
            cs_ref[s] = extc[pc + n - hc:pc + n, :]
            ds_ref[s] = extd[pd + n - hd:pd + n, :]

    for b in range(tile // A_BLOCK):
        rows = slice(b * A_BLOCK, (b + 1) * A_BLOCK)
        c = _layer_norm(conv_ref[rows, :], lng_ref[...], lnb_ref[...])
        ycat_ref[rows, 0:cwid] = (c * jax.nn.sigmoid(c)).astype(BF16)

    y = jnp.dot(ycat_ref[...], wout_ref[...], preferred_element_type=F32)
    xo_ref[...] = x_ref[...] + y


def _odd_mixer(geo, layer_idx, x, g, w_in, w_out, conv_w, conv_b, ln_g, ln_b, dconv_w, state_c, state_d):
    n, d = x.shape
    tile = geo.tile
    in_w = w_in.shape[1]
    cwid = w_out.shape[0] // 2
    kc, kd = conv_w.shape[0], dconv_w.shape[0]
    hc, hd = kc - 1, kd - 1
    pc, pd = _round_up(hc, SUBLANES), _round_up(hd, SUBLANES)
    pt = geo.prompt_tiles
    n_sample_seq = state_c.shape[1]
    spt = geo.seqs_per_tile

    def sample_tile(i):
        return jnp.maximum(i - pt, 0)

    def prompt_seq(i):
        return jnp.minimum(i // geo.tiles_per_seq, geo.n_prompt_seq - 1)

    in_specs = [
        pl.BlockSpec((tile, d), lambda i: (i, 0)),
        _resident((1, d)),
        _resident(w_in.shape),
        _resident(w_out.shape),
        _resident(conv_w.shape),
        _resident((1, cwid)),
        _resident((1, cwid)),
        _resident((1, cwid)),
        _resident(dconv_w.shape),
        pl.BlockSpec((None, spt, hc, cwid), lambda i: (layer_idx, sample_tile(i), 0, 0)),
        pl.BlockSpec((None, spt, hd, cwid), lambda i: (layer_idx, sample_tile(i), 0, 0)),
    ]
    out_shape = [
        jax.ShapeDtypeStruct((n, d), F32),
        jax.ShapeDtypeStruct((geo.n_prompt_seq, hc, cwid), F32),
        jax.ShapeDtypeStruct((geo.n_prompt_seq, hd, cwid), F32),
        jax.ShapeDtypeStruct((n_sample_seq, hc, cwid), F32),
        jax.ShapeDtypeStruct((n_sample_seq, hd, cwid), F32),
    ]
    out_specs = [
        pl.BlockSpec((tile, d), lambda i: (i, 0)),
        pl.BlockSpec((1, hc, cwid), lambda i: (prompt_seq(i), 0, 0)),
        pl.BlockSpec((1, hd, cwid), lambda i: (prompt_seq(i), 0, 0)),
        pl.BlockSpec((spt, hc, cwid), lambda i: (sample_tile(i), 0, 0)),
        pl.BlockSpec((spt, hd, cwid), lambda i: (sample_tile(i), 0, 0)),
    ]
    scratch = [
        ((tile, in_w), F32),
        ((pc + tile, cwid), F32),
        ((pd + tile, cwid), F32),
        ((spt, pc + geo.sample_len, cwid), F32),
        ((spt, pd + geo.sample_len, cwid), F32),
        ((tile, cwid), F32),
        ((tile, 2 * cwid), BF16),
    ]
    block_bytes = (2 * _nbytes((tile, d), F32)
                   + 3 * _nbytes((spt, pc, cwid), F32) + 3 * _nbytes((spt, pd, cwid), F32))
    resident_bytes = _nbytes(w_in.shape, BF16) + _nbytes(w_out.shape, BF16)
    scratch_bytes = sum(_nbytes(s, t) for s, t in scratch)
    temp_bytes = _nbytes((tile, in_w), F32) + _nbytes((tile, d), F32)

    return pl.pallas_call(
        functools.partial(_odd_mixer_kernel, geo),
        grid=(geo.steps,),
        in_specs=in_specs,
        out_specs=out_specs,
        out_shape=out_shape,
        scratch_shapes=[pltpu.VMEM(s, t) for s, t in scratch],
        compiler_params=pltpu.CompilerParams(
            dimension_semantics=("arbitrary",),
            vmem_limit_bytes=_vmem_limit(block_bytes, resident_bytes, scratch_bytes, temp_bytes)),
        name=f"odd_mixer_{layer_idx}",
    )(x, g, w_in, w_out, conv_w, conv_b, ln_g, ln_b, dconv_w, state_c, state_d)


def _ffn_kernel(geo, final, x_ref, g_ref, w1_ref, w2_ref, gf_ref, *out_refs):
    x = x_ref[...]
    h = _rms_norm(x, g_ref[...]).astype(BF16)
    acc = x
    for c0 in range(0, w1_ref.shape[1], FF_CHUNK):
        t = jnp.maximum(jnp.dot(h, w1_ref[:, c0:c0 + FF_CHUNK], preferred_element_type=F32), 0.0)
        acc = acc + jnp.dot((t * t).astype(BF16), w2_ref[c0:c0 + FF_CHUNK, :], preferred_element_type=F32)
    if not final:
        out_refs[0][...] = acc
        return
    y = _rms_norm(acc, gf_ref[...])
    is_prompt = pl.program_id(0) < geo.prompt_tiles

    @pl.when(is_prompt)
    def _():
        out_refs[0][...] = y

    @pl.when(jnp.logical_not(is_prompt))
    def _():
        out_refs[1][...] = y


def _ffn(geo, layer, x, g, w1, w2, g_final, final):
    n, d = x.shape
    tile = geo.tile
    pt = geo.prompt_tiles
    if final:
        out_shape = [jax.ShapeDtypeStruct((pt * tile, d), F32),
                     jax.ShapeDtypeStruct((geo.sample_tiles * tile, d), F32)]
        out_specs = [pl.BlockSpec((tile, d), lambda i: (jnp.minimum(i, pt - 1), 0)),
                     pl.BlockSpec((tile, d), lambda i: (jnp.maximum(i - pt, 0), 0))]
    else:
        out_shape = [jax.ShapeDtypeStruct((n, d), F32)]
        out_specs = [pl.BlockSpec((tile, d), lambda i: (i, 0))]
    block_bytes = (1 + len(out_shape)) * _nbytes((tile, d), F32)
    resident_bytes = _nbytes(w1.shape, BF16) + _nbytes(w2.shape, BF16)
    temp_bytes = 2 * _nbytes((tile, FF_CHUNK), F32) + 2 * _nbytes((tile, d), F32)
    return pl.pallas_call(
        functools.partial(_ffn_kernel, geo, final),
        grid=(geo.steps,),
        in_specs=[pl.BlockSpec((tile, d), lambda i: (i, 0)), _resident((1, d)),
                  _resident(w1.shape), _resident(w2.shape), _resident((1, d))],
        out_specs=out_specs,
        out_shape=out_shape,
        compiler_params=pltpu.CompilerParams(
            dimension_semantics=("arbitrary",),
            vmem_limit_bytes=_vmem_limit(block_bytes, resident_bytes, 0, temp_bytes)),
        name=f"ffn_{layer}",
    )(x, g, w1, w2, g_final)


def _gating_operands(w_s, b_s, sample_len):
    heads, blk, _ = w_s.shape
    cidx = jnp.arange(blk) // CHUNK
    mask = (cidx[None, :] <= cidx[:, None]).astype(w_s.dtype)
    w_prompt = w_s * mask[None]
    head_dim_lanes = LANES // 2
    reps = blk // sample_len
    eye = jnp.eye(reps, dtype=w_s.dtype)
    w_first = w_prompt[:, :sample_len, :sample_len]
    w_sample = jnp.einsum("ab,hij->haibj", eye, w_first).reshape(heads, blk, blk)
    b_sample = jnp.tile(b_s[:, :sample_len], (1, reps))

    def pairs(w):
        return jnp.concatenate([w[0::2], w[1::2]], axis=2)

    def bias(b):
        return jnp.repeat(b.T, head_dim_lanes, axis=1)

    wmix = jnp.stack([pairs(w_prompt), pairs(w_sample)]).astype(BF16)
    bmix = jnp.stack([bias(b_s), bias(b_sample)]).astype(F32)
    return wmix, bmix


def kernel(x_prompt, x_sample, state_pool, state_conv_c, state_conv_d, norm_mix_g, norm_ffn_g, final_norm_g,
           w_in_even, w_out_even, a_w_s, a_b_s, a_ln_g, a_ln_b, b_pool_w, b_scale,
           w_in_odd, w_out_odd, c_conv_w, c_conv_b, c_ln_g, c_ln_b, d_conv_w, w_ff1, w_ff2):
    n_prompt_seq, seq, d = x_prompt.shape
    n_sample_seq, sample_len, _ = x_sample.shape
    depth = norm_mix_g.shape[0]
    tile = ROW_TILE
    assert seq % tile == 0 and tile % A_BLOCK == 0 and tile % sample_len == 0
    assert A_BLOCK % sample_len == 0 and sample_len <= CHUNK
    assert (n_sample_seq * sample_len) % tile == 0
    assert a_w_s.shape[1] == A_HEADS and a_w_s.shape[2] == A_BLOCK
    assert w_out_even.shape[1] // 2 == A_HEADS * (LANES // 2)
    geo = Geometry(
        d_model=d, tile=tile,
        prompt_tiles=n_prompt_seq * seq // tile, tiles_per_seq=seq // tile, n_prompt_seq=n_prompt_seq,
        sample_tiles=n_sample_seq * sample_len // tile, sample_len=sample_len,
        seqs_per_tile=tile // sample_len)

    xs = (x_prompt.reshape(-1, d), x_sample.reshape(-1, d))

    def row(v):
        return v.reshape(1, -1)

    pools_p, pools_s, vs, cs_p, cs_s, ds_p, ds_s = [], [], [], [], [], [], []
    for layer in range(depth):
        li = layer // 2
        if layer % 2 == 0:
            wmix, bmix = _gating_operands(a_w_s[li], a_b_s[li], sample_len)
            x, v_s, pool_p, pool_s = _even_mixer(
                geo, li, xs, row(norm_mix_g[layer]), w_in_even[li].astype(BF16), w_out_even[li].astype(BF16),
                wmix, bmix, row(a_ln_g[li]), row(a_ln_b[li]), b_pool_w[li].astype(BF16), row(b_scale[li]),
                state_pool)
            vs.append(v_s.reshape(n_sample_seq, sample_len, -1))
            pools_p.append(pool_p)
            pools_s.append(pool_s)
        else:
            x, c_p, d_p, c_s, d_s = _odd_mixer(
                geo, li, xs[0], row(norm_mix_g[layer]), w_in_odd[li].astype(BF16), w_out_odd[li].astype(BF16),
                c_conv_w[li], row(c_conv_b[li]), row(c_ln_g[li]), row(c_ln_b[li]), d_conv_w[li],
                state_conv_c, state_conv_d)
            cs_p.append(c_p)
            ds_p.append(d_p)
            cs_s.append(c_s)
            ds_s.append(d_s)
        xs = _ffn(geo, layer, x, row(norm_ffn_g[layer]), w_ff1[layer].astype(BF16), w_ff2[layer].astype(BF16),
                  row(final_norm_g), final=(layer == depth - 1))

    y_prompt = xs[0].reshape(n_prompt_seq, seq, d)
    y_sample = xs[1].reshape(n_sample_seq, sample_len, d)
    return (y_prompt, y_sample, jnp.stack(pools_p), jnp.stack(cs_p), jnp.stack(ds_p),
            jnp.stack(vs), jnp.stack(pools_s), jnp.stack(cs_s), jnp.stack(ds_s))
```

```python
import functools
from typing import NamedTuple

import jax
import jax.numpy as jnp
from jax import lax
from jax.experimental import pallas as pl
from jax.experimental.pallas import tpu as pltpu

F32 = jnp.float32
BF16 = jnp.bfloat16

EPS = 1e-6
PAST_LEN = 4096
CHUNK = 64
A_BLOCK = 128
A_HEADS = 8
POOL_WINDOWS = (2, 4, 8, 16)
LANES = 128
SUBLANES = 8
VMEM_CAPACITY_BYTES = 64 * 1024 * 1024
ROW_TILE = 512
FF_CHUNK = 1024
CONV_ROWS = 64

POOL_HIST = max(POOL_WINDOWS) - 1


def _round_up(n, m):
    return -(-n // m) * m


POOL_PAD = _round_up(POOL_HIST, SUBLANES)


class Geometry(NamedTuple):
    d_model: int
    tile: int
    prompt_tiles: int
    tiles_per_seq: int
    n_prompt_seq: int
    sample_tiles: int
    sample_len: int
    seqs_per_tile: int

    @property
    def tiles(self):
        return self.prompt_tiles + self.sample_tiles

    @property
    def steps(self):
        return self.tiles + 1

    def mix_tile(self, s):
        return jnp.minimum(s, self.tiles - 1)

    def ffn_tile(self, s):
        return jnp.maximum(s - 1, 0)


def _resident(shape, index=None):
    full = (0,) * len(shape) if index is None else tuple(index)
    return pl.BlockSpec(shape, lambda s: full, pipeline_mode=pl.Buffered(1))


def _spec_bytes(spec, dtype):
    return _nbytes([b for b in spec.block_shape if b is not None], dtype)


def _nbytes(shape, dtype):
    n = 1
    for s in shape:
        n *= s
    return n * jnp.dtype(dtype).itemsize


def _vmem_limit(block_bytes, resident_bytes, scratch_bytes, temp_bytes):
    need = 2 * block_bytes + resident_bytes + scratch_bytes + temp_bytes
    assert need <= VMEM_CAPACITY_BYTES, need
    return min(VMEM_CAPACITY_BYTES - (2 << 20), _round_up(need + need // 8, 1 << 20))


def _rms_norm(x, g):
    y = x * lax.rsqrt(jnp.mean(x * x, axis=-1, keepdims=True) + EPS)
    return y * g


def _layer_norm(x, g, b):
    mu = jnp.mean(x, axis=-1, keepdims=True)
    xc = x - mu
    y = xc * lax.rsqrt(jnp.mean(xc * xc, axis=-1, keepdims=True) + EPS)
    return y * g + b


def _gelu_tanh(x):
    c = 0.7978845608028654
    return x * (0.5 * (1.0 + jnp.tanh(c * (x + 0.044715 * (x * x * x)))))


def _load_rows(x_refs, is_prompt):
    if len(x_refs) == 1:
        return x_refs[0][...]
    return jnp.where(is_prompt, x_refs[0][...], x_refs[1][...])


def _tile_ids(geo):
    step = pl.program_id(0)
    m = geo.mix_tile(step)
    return step, m < geo.prompt_tiles, m % geo.tiles_per_seq


def _ffn_stages(x1_ref, g_ref, w1_ref, w2_ref, acc_ref, h_ref, t_ref):
    n_chunks = w1_ref.shape[1] // FF_CHUNK

    def up(c):
        def run():
            if c == 0:
                x = x1_ref[...]
                h_ref[...] = _rms_norm(x, g_ref[...]).astype(BF16)
                acc_ref[...] = x
            cols = slice(c * FF_CHUNK, (c + 1) * FF_CHUNK)
            t = jnp.maximum(jnp.dot(h_ref[...], w1_ref[:, cols], preferred_element_type=F32), 0.0)
            t_ref[...] = (t * t).astype(BF16)
        return run

    def down(c):
        def run():
            cols = slice(c * FF_CHUNK, (c + 1) * FF_CHUNK)
            acc_ref[...] += jnp.dot(t_ref[...], w2_ref[cols, :], preferred_element_type=F32)
        return run

    return [stage(c) for c in range(n_chunks) for stage in (up, down)]


def _run_interleaved(*stage_lists):
    order = sorted(((k + 0.5) / len(stages), n, k) for n, stages in enumerate(stage_lists) for k in range(len(stages)))
    for _, n, k in order:
        stage_lists[n][k]()


def _project(x_refs, is_prompt, g_ref, win_ref, xres_ref, z_ref):
    x = _load_rows(x_refs, is_prompt)
    xres_ref[...] = x
    z_ref[...] = jnp.dot(_rms_norm(x, g_ref[...]).astype(BF16), win_ref[...], preferred_element_type=F32)


def _emit_rows(geo, step, final, acc_ref, gfin_ref, out_refs):
    if not final:
        out_refs[0][...] = acc_ref[...]
        return
    is_prompt = geo.ffn_tile(step) < geo.prompt_tiles

    @pl.when(is_prompt)
    def _():
        out_refs[0][...] = _rms_norm(acc_ref[...], gfin_ref[...])

    @pl.when(jnp.logical_not(is_prompt))
    def _():
        out_refs[1][...] = _rms_norm(acc_ref[...], gfin_ref[...])


def _pool_segment(ext, length, pos0, out_ref, out_base):
    pos = pos0 + lax.broadcasted_iota(jnp.int32, (length, 1), 0)
    for g, w in enumerate(POOL_WINDOWS):
        lanes = slice(g * LANES, (g + 1) * LANES)
        win = ext[0:POOL_PAD + length, lanes]
        s = win
        span = 1
        while span < w:
            s = s + pltpu.roll(s, span, axis=0)
            span *= 2
        inv_cnt = 1.0 / jnp.minimum(w, pos + 1).astype(F32)
        out_ref[out_base:out_base + length, lanes] = s[POOL_PAD:] * inv_cnt - win[POOL_PAD:]


def _even_layer_kernel(geo, n_x, final, *refs):
    n_out = 2 if final else 1
    x_refs = refs[:n_x]
    (g_ref, win_ref, wout_ref, wmix_ref, bmix_ref, lng_ref, lnb_ref, pw_ref, bscale_ref, hist_ref,
     gf_ref, w1_ref, w2_ref, gfin_ref) = refs[n_x:n_x + 14]
    out_refs = refs[n_x + 14:n_x + 14 + n_out]
    (v_ref, poolp_ref, pools_ref,
     x1_ref, acc_ref, z_ref, xres_ref, h_ref, t_ref, ext_ref, carry_ref, pooled_ref, ycat_ref,
     hist_scr, v_scr) = refs[n_x + 14 + n_out:]
    step, is_prompt, j = _tile_ids(geo)
    tile, seg = geo.tile, geo.sample_len
    aw = wout_ref.shape[0] // 2

    @pl.when(step == 0)
    def _():
        x1_ref[...] = jnp.zeros(x1_ref.shape, F32)
        carry_ref[...] = jnp.zeros(carry_ref.shape, F32)

    ffn = _ffn_stages(x1_ref, gf_ref, w1_ref, w2_ref, acc_ref, h_ref, t_ref)
    _project(x_refs, is_prompt, g_ref, win_ref, xres_ref, z_ref)
    hist_scr[...] = hist_ref[...]

    fresh = j == 0

    def pool_stage(s):
        def run():
            rows = slice(s * seg, (s + 1) * seg)
            if s == 0:
                before = jnp.where(fresh, 0.0, carry_ref[...])
            else:
                before = z_ref[s * seg - POOL_PAD:s * seg, 2 * aw:]
            ext_ref[s, 0:POOL_PAD, :] = jnp.where(is_prompt, before, hist_scr[s])
            ext_ref[s, POOL_PAD:, :] = z_ref[rows, 2 * aw:]
            pos0 = jnp.where(is_prompt, j * tile + s * seg, PAST_LEN)
            _pool_segment(ext_ref.at[s], seg, pos0, pooled_ref, s * seg)
        return run

    lane = lax.broadcasted_iota(jnp.int32, (A_BLOCK, LANES), 1)
    low_head = lane < (LANES // 2)
    n_pairs = aw // LANES

    def gate_stage(b):
        def run():
            rows = slice(b * A_BLOCK, (b + 1) * A_BLOCK)
            v = _layer_norm(_gelu_tanh(z_ref[rows, aw:2 * aw]), lng_ref[...], lnb_ref[...])
            v_scr[rows, :] = v
            mixed = []
            for p in range(n_pairs):
                vp = v[:, p * LANES:(p + 1) * LANES].astype(BF16)
                zero = jnp.zeros_like(vp)
                rhs = jnp.concatenate([jnp.where(low_head, vp, zero), jnp.where(low_head, zero, vp)], axis=0)
                mixed.append(jnp.dot(wmix_ref[0, p], rhs, preferred_element_type=F32))
            mixed = jnp.concatenate(mixed, axis=1) + bmix_ref[0]
            u = _gelu_tanh(z_ref[rows, 0:aw])
            ycat_ref[rows, 0:aw] = (u * mixed).astype(BF16)
        return run

    mixer = ([pool_stage(s) for s in range(geo.seqs_per_tile)]
             + [gate_stage(b) for b in range(tile // A_BLOCK)])
    _run_interleaved(mixer, ffn)
    carry_ref[...] = z_ref[tile - POOL_PAD:tile, 2 * aw:]

    for g in range(len(POOL_WINDOWS)):
        lanes = slice(g * LANES, (g + 1) * LANES)
        yb = jnp.dot(pooled_ref[:, lanes].astype(BF16), pw_ref[g], preferred_element_type=F32)
        ycat_ref[:, aw + g * LANES:aw + (g + 1) * LANES] = (yb * bscale_ref[:, lanes]).astype(BF16)

    y = jnp.dot(ycat_ref[...], wout_ref[...], preferred_element_type=F32)
    x1_ref[...] = xres_ref[...] + y

    tail = slice(POOL_PAD + seg - POOL_HIST, POOL_PAD + seg)
    v_ref[...] = v_scr[...]
    for s in range(geo.seqs_per_tile):
        pools_ref[s] = ext_ref[s, tail, :]
    poolp_ref[0] = ext_ref[geo.seqs_per_tile - 1, tail, :]
    _emit_rows(geo, step, final, acc_ref, gfin_ref, out_refs)


def _dwconv_rows(ext, first_row, w_ref, n_taps, r0, length, lanes):
    wlen = length + _round_up(first_row + n_taps - 1, SUBLANES)
    win = ext[r0:r0 + wlen, lanes]
    acc = None
    for r in range(SUBLANES):
        taps = [k for k in range(n_taps) if (first_row + k) % SUBLANES == r]
        if not taps:
            continue
        shifted = win if r == 0 else pltpu.roll(win, wlen - r, axis=0)
        for k in taps:
            q0 = (first_row + k) // SUBLANES * SUBLANES
            term = shifted[q0:q0 + length] * w_ref[k:k + 1, lanes]
            acc = term if acc is None else acc + term
    return acc


def _odd_layer_kernel(geo, n_x, final, *refs):
    n_out = 2 if final else 1
    x_refs = refs[:n_x]
    (g_ref, win_ref, wout_ref, cw_ref, cb_ref, lng_ref, lnb_ref, dw_ref, histc_ref, histd_ref,
     gf_ref, w1_ref, w2_ref, gfin_ref) = refs[n_x:n_x + 14]
    out_refs = refs[n_x + 14:n_x + 14 + n_out]
    (cp_ref, dp_ref, cs_ref, ds_ref,
     x1_ref, acc_ref, z_ref, xres_ref, h_ref, t_ref, extc_ref, extd_ref, carryc_ref, carryd_ref, conv_ref,
     ycat_ref, histc_scr, histd_scr) = refs[n_x + 14 + n_out:]
    step, is_prompt, j = _tile_ids(geo)
    tile, seg = geo.tile, geo.sample_len
    cwid = wout_ref.shape[0] // 2
    kc, kd = cw_ref.shape[0], dw_ref.shape[0]
    hc, hd = kc - 1, kd - 1
    pc, pd = carryc_ref.shape[0], carryd_ref.shape[0]
    o = 2 * cwid

    @pl.when(step == 0)
    def _():
        x1_ref[...] = jnp.zeros(x1_ref.shape, F32)
        carryc_ref[...] = jnp.zeros(carryc_ref.shape, F32)
        carryd_ref[...] = jnp.zeros(carryd_ref.shape, F32)

    ffn = _ffn_stages(x1_ref, gf_ref, w1_ref, w2_ref, acc_ref, h_ref, t_ref)
    _project(x_refs, is_prompt, g_ref, win_ref, xres_ref, z_ref)
    histc_scr[...] = histc_ref[...]
    histd_scr[...] = histd_ref[...]

    fresh = j == 0

    def conv_stage(s):
        def run():
            rows = slice(s * seg, (s + 1) * seg)
            c_in = z_ref[rows, 0:cwid] * jax.nn.sigmoid(z_ref[rows, cwid:2 * cwid])
            d_in = z_ref[rows, o + cwid:o + 2 * cwid] * z_ref[rows, o + 2 * cwid:o + 3 * cwid]
            if s == 0:
                prev_c = jnp.where(fresh, 0.0, carryc_ref[...])
                prev_d = jnp.where(fresh, 0.0, carryd_ref[...])
            else:
                prev_c = extc_ref[s - 1, seg:, :]
                prev_d = extd_ref[s - 1, seg:, :]
            extc_ref[s, 0:pc, :] = jnp.where(is_prompt, prev_c, histc_scr[s])
            extd_ref[s, 0:pd, :] = jnp.where(is_prompt, prev_d, histd_scr[s])
            extc_ref[s, pc:, :] = c_in
            extd_ref[s, pd:, :] = d_in
            extc, extd = extc_ref.at[s], extd_ref.at[s]
            for lt in range(cwid // LANES):
                lanes = slice(lt * LANES, (lt + 1) * LANES)
                for r0 in range(0, seg, CONV_ROWS):
                    out_rows = slice(s * seg + r0, s * seg + r0 + CONV_ROWS)
                    conv_ref[out_rows, lanes] = (_dwconv_rows(extc, pc - hc, cw_ref, kc, r0, CONV_ROWS, lanes)
                                                 + cb_ref[:, lanes])
                    d = _dwconv_rows(extd, pd - hd, dw_ref, kd, r0, CONV_ROWS, lanes)
                    ycat_ref[out_rows, cwid + lt * LANES:cwid + (lt + 1) * LANES] = (
                        z_ref[out_rows, o + lt * LANES:o + (lt + 1) * LANES] * d).astype(BF16)
        return run

    def norm_stage(b):
        def run():
            rows = slice(b * A_BLOCK, (b + 1) * A_BLOCK)
            c = _layer_norm(conv_ref[rows, :], lng_ref[...], lnb_ref[...])
            ycat_ref[rows, 0:cwid] = (c * jax.nn.sigmoid(c)).astype(BF16)
        return run

    mixer = ([conv_stage(s) for s in range(geo.seqs_per_tile)]
             + [norm_stage(b) for b in range(tile // A_BLOCK)])
    _run_interleaved(mixer, ffn)
    last = geo.seqs_per_tile - 1
    carryc_ref[...] = extc_ref[last, seg:, :]
    carryd_ref[...] = extd_ref[last, seg:, :]

    y = jnp.dot(ycat_ref[...], wout_ref[...], preferred_element_type=F32)
    x1_ref[...] = xres_ref[...] + y

    tail_c, tail_d = slice(pc + seg - hc, pc + seg), slice(pd + seg - hd, pd + seg)
    for s in range(geo.seqs_per_tile):
        cs_ref[s] = extc_ref[s, tail_c, :]
        ds_ref[s] = extd_ref[s, tail_d, :]
    cp_ref[0] = extc_ref[last, tail_c, :]
    dp_ref[0] = extd_ref[last, tail_d, :]
    _emit_rows(geo, step, final, acc_ref, gfin_ref, out_refs)


def _layer_call(geo, name, body, xs, final, mixer_inputs, mixer_specs, ffn_inputs,
                mixer_out_shapes, mixer_out_specs, mixer_scratch, in_w):
    d, tile, pt = geo.d_model, geo.tile, geo.prompt_tiles
    g_ffn, w1, w2, g_final = ffn_inputs
    ffn_tile = geo.ffn_tile
    if len(xs) == 1:
        x_specs = [pl.BlockSpec((tile, d), lambda s: (geo.mix_tile(s), 0))]
    else:
        x_specs = [pl.BlockSpec((tile, d), lambda s: (jnp.minimum(s, pt - 1), 0)),
                   pl.BlockSpec((tile, d), lambda s: (jnp.clip(s - pt, 0, geo.sample_tiles - 1), 0))]
    ffn_specs = [_resident((1, d)), _resident((None,) + w1.shape[1:], (name[1], 0, 0)),
                 _resident((None,) + w2.shape[1:], (name[1], 0, 0)), _resident((1, d))]
    if final:
        row_shapes = [jax.ShapeDtypeStruct((pt * tile, d), F32),
                      jax.ShapeDtypeStruct((geo.sample_tiles * tile, d), F32)]
        row_specs = [pl.BlockSpec((tile, d), lambda s: (jnp.minimum(ffn_tile(s), pt - 1), 0)),
                     pl.BlockSpec((tile, d), lambda s: (jnp.maximum(ffn_tile(s) - pt, 0), 0))]
    else:
        row_shapes = [jax.ShapeDtypeStruct((geo.tiles * tile, d), F32)]
        row_specs = [pl.BlockSpec((tile, d), lambda s: (ffn_tile(s), 0))]
    scratch = [((tile, d), F32),
               ((tile, d), F32),
               ((tile, in_w), F32),
               ((tile, d), F32),
               ((tile, d), BF16),
               ((tile, FF_CHUNK), BF16)] + mixer_scratch
    mixer_io = list(zip(mixer_specs, mixer_inputs)) + list(zip(mixer_out_specs, mixer_out_shapes))
    block_bytes = ((len(xs) + len(row_shapes)) * _nbytes((tile, d), F32)
                   + sum(_spec_bytes(sp, a.dtype) for sp, a in mixer_io if sp.pipeline_mode is None))
    resident_bytes = (_nbytes(w1.shape[1:], w1.dtype) + _nbytes(w2.shape[1:], w2.dtype)
                      + sum(_spec_bytes(sp, a.dtype) for sp, a in mixer_io if sp.pipeline_mode is not None))
    scratch_bytes = sum(_nbytes(s, t) for s, t in scratch)
    temp_bytes = _nbytes((tile, FF_CHUNK), F32) + 2 * _nbytes((tile, d), F32)

    return pl.pallas_call(
        functools.partial(body, geo, len(xs), final),
        grid=(geo.steps,),
        in_specs=x_specs + mixer_specs + ffn_specs,
        out_specs=row_specs + mixer_out_specs,
        out_shape=row_shapes + mixer_out_shapes,
        scratch_shapes=[pltpu.VMEM(s, t) for s, t in scratch],
        compiler_params=pltpu.CompilerParams(
            dimension_semantics=("arbitrary",),
            vmem_limit_bytes=_vmem_limit(block_bytes, resident_bytes, scratch_bytes, temp_bytes)),
        name=f"{name[0]}_layer_{name[1]}",
    )(*xs, *mixer_inputs, g_ffn, w1, w2, g_final)


def _segment_maps(geo):
    pt = geo.prompt_tiles

    def sample_tile(s):
        return jnp.maximum(geo.mix_tile(s) - pt, 0)

    def prompt_seq(s):
        i = geo.mix_tile(s)
        return jnp.where(i < pt, i // geo.tiles_per_seq, geo.n_prompt_seq)

    return sample_tile, prompt_seq


def _even_layer(geo, layer, xs, final, g_mix, w_in, w_out, wmix, bmix, ln_g, ln_b, pool_w, b_scale,
                hist_pool, ffn_inputs):
    li = layer // 2
    tile, spt, seg, pt = geo.tile, geo.seqs_per_tile, geo.sample_len, geo.prompt_tiles
    in_w = w_in.shape[2]
    aw = w_out.shape[1] // 2
    bw = in_w - 2 * aw
    n_sample_seq = hist_pool.shape[1]
    sample_tile, prompt_seq = _segment_maps(geo)

    mixer_specs = [
        _resident((1, geo.d_model)),
        _resident((None,) + w_in.shape[1:], (li, 0, 0)),
        _resident((None,) + w_out.shape[1:], (li, 0, 0)),
        pl.BlockSpec((1,) + wmix.shape[1:], lambda s: (geo.mix_tile(s) // pt, 0, 0, 0)),
        pl.BlockSpec((1,) + bmix.shape[1:], lambda s: (geo.mix_tile(s) // pt, 0, 0)),
        _resident((1, aw)),
        _resident((1, aw)),
        _resident((None,) + pool_w.shape[1:], (li, 0, 0, 0)),
        _resident((1, bw)),
        pl.BlockSpec((None, spt, POOL_PAD, bw), lambda s: (li, sample_tile(s), 0, 0)),
    ]
    out_shapes = [
        jax.ShapeDtypeStruct((geo.sample_tiles * tile, aw), F32),
        jax.ShapeDtypeStruct((geo.n_prompt_seq + 1, POOL_HIST, bw), F32),
        jax.ShapeDtypeStruct((n_sample_seq, POOL_HIST, bw), F32),
    ]
    out_specs = [
        pl.BlockSpec((tile, aw), lambda s: (sample_tile(s), 0)),
        pl.BlockSpec((1, POOL_HIST, bw), lambda s: (prompt_seq(s), 0, 0)),
        pl.BlockSpec((spt, POOL_HIST, bw), lambda s: (sample_tile(s), 0, 0)),
    ]
    scratch = [
        ((spt, POOL_PAD + seg, bw), F32),
        ((POOL_PAD, bw), F32),
        ((tile, bw), F32),
        ((tile, aw + bw), BF16),
        ((spt, POOL_PAD, bw), F32),
        ((tile, aw), F32),
    ]
    outs = _layer_call(
        geo, ("even", layer), _even_layer_kernel, xs, final,
        (g_mix, w_in, w_out, wmix, bmix, ln_g, ln_b, pool_w, b_scale, hist_pool), mixer_specs, ffn_inputs,
        out_shapes, out_specs, scratch, in_w)
    n_rows = 2 if final else 1
    v_s, pool_p, pool_s = outs[n_rows:]
    return outs[:n_rows], v_s, pool_p[:geo.n_prompt_seq], pool_s


def _odd_layer(geo, layer, xs, final, g_mix, w_in, w_out, conv_w, conv_b, ln_g, ln_b, dconv_w,
               hist_c, hist_d, ffn_inputs):
    li = layer // 2
    tile, spt, seg = geo.tile, geo.seqs_per_tile, geo.sample_len
    in_w = w_in.shape[2]
    cwid = w_out.shape[1] // 2
    kc, kd = conv_w.shape[1], dconv_w.shape[1]
    hc, hd = kc - 1, kd - 1
    pc, pd = hist_c.shape[2], hist_d.shape[2]
    n_sample_seq = hist_c.shape[1]
    sample_tile, prompt_seq = _segment_maps(geo)

    mixer_specs = [
        _resident((1, geo.d_model)),
        _resident((None,) + w_in.shape[1:], (li, 0, 0)),
        _resident((None,) + w_out.shape[1:], (li, 0, 0)),
        _resident((None,) + conv_w.shape[1:], (li, 0, 0)),
        _resident((1, cwid)),
        _resident((1, cwid)),
        _resident((1, cwid)),
        _resident((None,) + dconv_w.shape[1:], (li, 0, 0)),
        pl.BlockSpec((None, spt, pc, cwid), lambda s: (li, sample_tile(s), 0, 0)),
        pl.BlockSpec((None, spt, pd, cwid), lambda s: (li, sample_tile(s), 0, 0)),
    ]
    out_shapes = [
        jax.ShapeDtypeStruct((geo.n_prompt_seq + 1, hc, cwid), F32),
        jax.ShapeDtypeStruct((geo.n_prompt_seq + 1, hd, cwid), F32),
        jax.ShapeDtypeStruct((n_sample_seq, hc, cwid), F32),
        jax.ShapeDtypeStruct((n_sample_seq, hd, cwid), F32),
    ]
    out_specs = [
        pl.BlockSpec((1, hc, cwid), lambda s: (prompt_seq(s), 0, 0)),
        pl.BlockSpec((1, hd, cwid), lambda s: (prompt_seq(s), 0, 0)),
        pl.BlockSpec((spt, hc, cwid), lambda s: (sample_tile(s), 0, 0)),
        pl.BlockSpec((spt, hd, cwid), lambda s: (sample_tile(s), 0, 0)),
    ]
    scratch = [
        ((spt, pc + seg, cwid), F32),
        ((spt, pd + seg, cwid), F32),
        ((pc, cwid), F32),
        ((pd, cwid), F32),
        ((tile, cwid), F32),
        ((tile, 2 * cwid), BF16),
        ((spt, pc, cwid), F32),
        ((spt, pd, cwid), F32),
    ]
    outs = _layer_call(
        geo, ("odd", layer), _odd_layer_kernel, xs, final,
        (g_mix, w_in, w_out, conv_w, conv_b, ln_g, ln_b, dconv_w, hist_c, hist_d), mixer_specs, ffn_inputs,
        out_shapes, out_specs, scratch, in_w)
    n_rows = 2 if final else 1
    c_p, d_p, c_s, d_s = outs[n_rows:]
    return outs[:n_rows], c_p[:geo.n_prompt_seq], d_p[:geo.n_prompt_seq], c_s, d_s


def _gating_operands(w_s, b_s, sample_len):
    heads, blk, _ = w_s.shape
    cidx = jnp.arange(blk) // CHUNK
    mask = (cidx[None, :] <= cidx[:, None]).astype(w_s.dtype)
    w_prompt = w_s * mask[None]
    head_dim_lanes = LANES // 2
    reps = blk // sample_len
    eye = jnp.eye(reps, dtype=w_s.dtype)
    w_first = w_prompt[:, :sample_len, :sample_len]
    w_sample = jnp.einsum("ab,hij->haibj", eye, w_first).reshape(heads, blk, blk)
    b_sample = jnp.tile(b_s[:, :sample_len], (1, reps))

    def pairs(w):
        return jnp.concatenate([w[0::2], w[1::2]], axis=2)

    def bias(b):
        return jnp.repeat(b.T, head_dim_lanes, axis=1)

    wmix = jnp.stack([pairs(w_prompt), pairs(w_sample)]).astype(BF16)
    bmix = jnp.stack([bias(b_s), bias(b_sample)]).astype(F32)
    return wmix, bmix


def _pad_history(state):
    rows = state.shape[2]
    return jnp.pad(state, ((0, 0), (0, 0), (_round_up(rows, SUBLANES) - rows, 0), (0, 0)))


def kernel(x_prompt, x_sample, state_pool, state_conv_c, state_conv_d, norm_mix_g, norm_ffn_g, final_norm_g,
           w_in_even, w_out_even, a_w_s, a_b_s, a_ln_g, a_ln_b, b_pool_w, b_scale,
           w_in_odd, w_out_odd, c_conv_w, c_conv_b, c_ln_g, c_ln_b, d_conv_w, w_ff1, w_ff2):
    n_prompt_seq, seq, d = x_prompt.shape
    n_sample_seq, sample_len, _ = x_sample.shape
    depth = norm_mix_g.shape[0]
    tile = ROW_TILE
    assert seq % tile == 0 and tile % A_BLOCK == 0 and tile % sample_len == 0
    assert A_BLOCK % sample_len == 0 and sample_len <= CHUNK
    assert (n_sample_seq * sample_len) % tile == 0
    assert a_w_s.shape[1] == A_HEADS and a_w_s.shape[2] == A_BLOCK
    assert w_out_even.shape[1] // 2 == A_HEADS * (LANES // 2)
    assert state_pool.shape[2] == POOL_HIST and sample_len >= _round_up(c_conv_w.shape[1] - 1, SUBLANES)
    geo = Geometry(
        d_model=d, tile=tile,
        prompt_tiles=n_prompt_seq * seq // tile, tiles_per_seq=seq // tile, n_prompt_seq=n_prompt_seq,
        sample_tiles=n_sample_seq * sample_len // tile, sample_len=sample_len,
        seqs_per_tile=tile // sample_len)

    def row(v):
        return v.reshape(1, -1)

    w_in_even_b, w_out_even_b = w_in_even.astype(BF16), w_out_even.astype(BF16)
    w_in_odd_b, w_out_odd_b = w_in_odd.astype(BF16), w_out_odd.astype(BF16)
    pool_w_b = b_pool_w.astype(BF16)
    w_ff1_b, w_ff2_b = w_ff1.astype(BF16), w_ff2.astype(BF16)
    hist_pool, hist_c, hist_d = _pad_history(state_pool), _pad_history(state_conv_c), _pad_history(state_conv_d)

    xs = (x_prompt.reshape(-1, d), x_sample.reshape(-1, d))
    pools_p, pools_s, vs, cs_p, cs_s, ds_p, ds_s = [], [], [], [], [], [], []
    for layer in range(depth):
        li = layer // 2
        final = layer == depth - 1
        ffn_inputs = (row(norm_ffn_g[layer]), w_ff1_b, w_ff2_b, row(final_norm_g))
        if layer % 2 == 0:
            wmix, bmix = _gating_operands(a_w_s[li], a_b_s[li], sample_len)
            xs, v_s, pool_p, pool_s = _even_layer(
                geo, layer, xs, final, row(norm_mix_g[layer]), w_in_even_b, w_out_even_b, wmix, bmix,
                row(a_ln_g[li]), row(a_ln_b[li]), pool_w_b, row(b_scale[li]), hist_pool, ffn_inputs)
            vs.append(v_s.reshape(n_sample_seq, sample_len, -1))
            pools_p.append(pool_p)
            pools_s.append(pool_s)
        else:
            xs, c_p, d_p, c_s, d_s = _odd_layer(
                geo, layer, xs, final, row(norm_mix_g[layer]), w_in_odd_b, w_out_odd_b, c_conv_w,
                row(c_conv_b[li]), row(c_ln_g[li]), row(c_ln_b[li]), d_conv_w, hist_c, hist_d, ffn_inputs)
            cs_p.append(c_p)
            ds_p.append(d_p)
            cs_s.append(c_s)
            ds_s.append(d_s)

    y_prompt = xs[0].reshape(n_prompt_seq, seq, d)
    y_sample = xs[1].reshape(n_sample_seq, sample_len, d)
    return (y_prompt, y_sample, jnp.stack(pools_p), jnp.stack(cs_p), jnp.stack(ds_p),
            jnp.stack(vs), jnp.stack(pools_s), jnp.stack(cs_s), jnp.stack(ds_s))
```

```python
import functools
from typing import NamedTuple

import jax
import jax.numpy as jnp
from jax import lax
from jax.experimental import pallas as pl
from jax.experimental.pallas import tpu as pltpu

F32 = jnp.float32
BF16 = jnp.bfloat16

EPS = 1e-6
PAST_LEN = 4096
CHUNK = 64
A_BLOCK = 128
A_HEADS = 8
POOL_WINDOWS = (2, 4, 8, 16)
LANES = 128
SUBLANES = 8
VMEM_CAPACITY_BYTES = 64 * 1024 * 1024
ROW_TILE = 512
FF_CHUNK = 1024
CONV_ROWS = 64

POOL_HIST = max(POOL_WINDOWS) - 1


def _round_up(n, m):
    return -(-n // m) * m


POOL_PAD = _round_up(POOL_HIST, SUBLANES)


class Geometry(NamedTuple):
    d_model: int
    tile: int
    prompt_tiles: int
    tiles_per_seq: int
    n_prompt_seq: int
    sample_tiles: int
    sample_len: int
    seqs_per_tile: int

    @property
    def tiles(self):
        return self.prompt_tiles + self.sample_tiles

    @property
    def steps(self):
        return self.tiles + 1

    def mix_tile(self, s):
        return jnp.minimum(s, self.tiles - 1)

    def ffn_tile(self, s):
        return jnp.maximum(s - 1, 0)


def _resident(shape, index=None):
    full = (0,) * len(shape) if index is None else tuple(index)
    return pl.BlockSpec(shape, lambda s: full, pipeline_mode=pl.Buffered(1))


def _spec_bytes(spec, dtype):
    return _nbytes([b for b in spec.block_shape if b is not None], dtype)


def _nbytes(shape, dtype):
    n = 1
    for s in shape:
        n *= s
    return n * jnp.dtype(dtype).itemsize


def _vmem_limit(block_bytes, resident_bytes, scratch_bytes, temp_bytes):
    need = 2 * block_bytes + resident_bytes + scratch_bytes + temp_bytes
    assert need <= VMEM_CAPACITY_BYTES, need
    return min(VMEM_CAPACITY_BYTES - (2 << 20), _round_up(need + need // 8, 1 << 20))


def _rms_norm(x, g):
    y = x * lax.rsqrt(jnp.mean(x * x, axis=-1, keepdims=True) + EPS)
    return y * g


def _layer_norm(x, g, b):
    mu = jnp.mean(x, axis=-1, keepdims=True)
    xc = x - mu
    y = xc * lax.rsqrt(jnp.mean(xc * xc, axis=-1, keepdims=True) + EPS)
    return y * g + b


def _gelu_tanh(x):
    c = 0.7978845608028654
    return x * (0.5 * (1.0 + jnp.tanh(c * (x + 0.044715 * (x * x * x)))))


def _load_rows(x_refs, is_prompt):
    if len(x_refs) == 1:
        return x_refs[0][...]
    return jnp.where(is_prompt, x_refs[0][...], x_refs[1][...])


def _tile_ids(geo):
    step = pl.program_id(0)
    m = geo.mix_tile(step)
    return step, m < geo.prompt_tiles, m % geo.tiles_per_seq


def _derived_zero(x):
    u = pltpu.bitcast(x, jnp.uint32)
    u = lax.shift_right_logical(lax.shift_right_logical(u, jnp.uint32(16)), jnp.uint32(16))
    return u.astype(F32)


def _order_after(ref, token, lane_tiles=(0,)):
    rows = SUBLANES * (4 // jnp.dtype(ref.dtype).itemsize)
    z = _derived_zero(token)
    z = jnp.concatenate([z] * (rows // SUBLANES), axis=0).astype(ref.dtype)
    for lt in lane_tiles:
        idx = (slice(0, rows), slice(lt * LANES, (lt + 1) * LANES))
        ref[idx] = ref[idx] + z


def _ffn_stages(x1_ref, g_ref, w1_ref, w2_ref, acc_ref, h_ref, t_ref):
    n_chunks = w1_ref.shape[1] // FF_CHUNK

    def up(c):
        def run(after=None):
            if c == 0:
                x = x1_ref[...]
                h_ref[...] = _rms_norm(x, g_ref[...]).astype(BF16)
                acc_ref[...] = x
            if after is not None:
                _order_after(h_ref, after)
            cols = slice(c * FF_CHUNK, (c + 1) * FF_CHUNK)
            t = jnp.maximum(jnp.dot(h_ref[...], w1_ref[:, cols], preferred_element_type=F32), 0.0)
            t_ref[...] = (t * t).astype(BF16)
            return t[-SUBLANES:, -LANES:]
        return run

    def down(c):
        def run(after=None):
            if after is not None:
                _order_after(t_ref, after)
            cols = slice(c * FF_CHUNK, (c + 1) * FF_CHUNK)
            d = jnp.dot(t_ref[...], w2_ref[cols, :], preferred_element_type=F32)
            acc_ref[...] += d
            return d[-SUBLANES:, -LANES:]
        return run

    return [stage(c) for c in range(n_chunks) for stage in (up, down)]


def _run_interleaved(*stage_lists):
    order = sorted(((k + 0.5) / len(stages), n, k) for n, stages in enumerate(stage_lists) for k in range(len(stages)))
    for _, n, k in order:
        stage_lists[n][k]()


def _project(x_refs, is_prompt, g_ref, win_ref, xres_ref, z_ref):
    x = _load_rows(x_refs, is_prompt)
    xres_ref[...] = x
    z_ref[...] = jnp.dot(_rms_norm(x, g_ref[...]).astype(BF16), win_ref[...], preferred_element_type=F32)


def _emit_rows(geo, step, final, acc_ref, gfin_ref, out_refs):
    if not final:
        out_refs[0][...] = acc_ref[...]
        return
    is_prompt = geo.ffn_tile(step) < geo.prompt_tiles

    @pl.when(is_prompt)
    def _():
        out_refs[0][...] = _rms_norm(acc_ref[...], gfin_ref[...])

    @pl.when(jnp.logical_not(is_prompt))
    def _():
        out_refs[1][...] = _rms_norm(acc_ref[...], gfin_ref[...])


def _pool_segment(ext, length, pos0, out_ref, out_base):
    pos = pos0 + lax.broadcasted_iota(jnp.int32, (length, 1), 0)
    for g, w in enumerate(POOL_WINDOWS):
        lanes = slice(g * LANES, (g + 1) * LANES)
        win = ext[0:POOL_PAD + length, lanes]
        s = win
        span = 1
        while span < w:
            s = s + pltpu.roll(s, span, axis=0)
            span *= 2
        inv_cnt = 1.0 / jnp.minimum(w, pos + 1).astype(F32)
        out_ref[out_base:out_base + length, lanes] = s[POOL_PAD:] * inv_cnt - win[POOL_PAD:]


def _even_layer_kernel(geo, n_x, final, *refs):
    n_out = 2 if final else 1
    x_refs = refs[:n_x]
    (g_ref, win_ref, wout_ref, wmix_ref, bmix_ref, lng_ref, lnb_ref, pw_ref, bscale_ref, hist_ref,
     gf_ref, w1_ref, w2_ref, gfin_ref) = refs[n_x:n_x + 14]
    out_refs = refs[n_x + 14:n_x + 14 + n_out]
    (v_ref, poolp_ref, pools_ref,
     x1_ref, acc_ref, z_ref, xres_ref, h_ref, t_ref, ext_ref, carry_ref, pooled_ref, ycat_ref,
     hist_scr, v_scr) = refs[n_x + 14 + n_out:]
    step, is_prompt, j = _tile_ids(geo)
    tile, seg = geo.tile, geo.sample_len
    aw = wout_ref.shape[0] // 2

    @pl.when(step == 0)
    def _():
        x1_ref[...] = jnp.zeros(x1_ref.shape, F32)
        carry_ref[...] = jnp.zeros(carry_ref.shape, F32)

    ffn = _ffn_stages(x1_ref, gf_ref, w1_ref, w2_ref, acc_ref, h_ref, t_ref)
    _project(x_refs, is_prompt, g_ref, win_ref, xres_ref, z_ref)
    hist_scr[...] = hist_ref[...]

    fresh = j == 0

    def pool_stage(s):
        def run():
            rows = slice(s * seg, (s + 1) * seg)
            if s == 0:
                before = jnp.where(fresh, 0.0, carry_ref[...])
            else:
                before = z_ref[s * seg - POOL_PAD:s * seg, 2 * aw:]
            ext_ref[s, 0:POOL_PAD, :] = jnp.where(is_prompt, before, hist_scr[s])
            ext_ref[s, POOL_PAD:, :] = z_ref[rows, 2 * aw:]
            pos0 = jnp.where(is_prompt, j * tile + s * seg, PAST_LEN)
            _pool_segment(ext_ref.at[s], seg, pos0, pooled_ref, s * seg)
        return run

    lane = lax.broadcasted_iota(jnp.int32, (A_BLOCK, LANES), 1)
    low_head = lane < (LANES // 2)
    n_pairs = aw // LANES

    def gate_stage(b):
        def run():
            rows = slice(b * A_BLOCK, (b + 1) * A_BLOCK)
            v = _layer_norm(_gelu_tanh(z_ref[rows, aw:2 * aw]), lng_ref[...], lnb_ref[...])
            v_scr[rows, :] = v
            mixed = []
            for p in range(n_pairs):
                vp = v[:, p * LANES:(p + 1) * LANES].astype(BF16)
                zero = jnp.zeros_like(vp)
                rhs = jnp.concatenate([jnp.where(low_head, vp, zero), jnp.where(low_head, zero, vp)], axis=0)
                mixed.append(jnp.dot(wmix_ref[0, p], rhs, preferred_element_type=F32))
            mixed = jnp.concatenate(mixed, axis=1) + bmix_ref[0]
            u = _gelu_tanh(z_ref[rows, 0:aw])
            ycat_ref[rows, 0:aw] = (u * mixed).astype(BF16)
        return run

    mixer = ([pool_stage(s) for s in range(geo.seqs_per_tile)]
             + [gate_stage(b) for b in range(tile // A_BLOCK)])
    _run_interleaved(mixer, ffn)
    carry_ref[...] = z_ref[tile - POOL_PAD:tile, 2 * aw:]

    for g in range(len(POOL_WINDOWS)):
        lanes = slice(g * LANES, (g + 1) * LANES)
        yb = jnp.dot(pooled_ref[:, lanes].astype(BF16), pw_ref[g], preferred_element_type=F32)
        ycat_ref[:, aw + g * LANES:aw + (g + 1) * LANES] = (yb * bscale_ref[:, lanes]).astype(BF16)

    y = jnp.dot(ycat_ref[...], wout_ref[...], preferred_element_type=F32)
    x1_ref[...] = xres_ref[...] + y

    tail = slice(POOL_PAD + seg - POOL_HIST, POOL_PAD + seg)
    v_ref[...] = v_scr[...]
    for s in range(geo.seqs_per_tile):
        pools_ref[s] = ext_ref[s, tail, :]
    poolp_ref[0] = ext_ref[geo.seqs_per_tile - 1, tail, :]
    _emit_rows(geo, step, final, acc_ref, gfin_ref, out_refs)


def _dwconv_rows(ext, first_row, w_ref, n_taps, r0, length, lanes):
    wlen = length + _round_up(first_row + n_taps - 1, SUBLANES)
    win = ext[r0:r0 + wlen, lanes]
    acc = None
    for r in range(SUBLANES):
        taps = [k for k in range(n_taps) if (first_row + k) % SUBLANES == r]
        if not taps:
            continue
        shifted = win if r == 0 else pltpu.roll(win, wlen - r, axis=0)
        for k in taps:
            q0 = (first_row + k) // SUBLANES * SUBLANES
            term = shifted[q0:q0 + length] * w_ref[k:k + 1, lanes]
            acc = term if acc is None else acc + term
    return acc


def _odd_layer_kernel(geo, n_x, final, *refs):
    n_out = 2 if final else 1
    x_refs = refs[:n_x]
    (g_ref, win_ref, wout_ref, cw_ref, cb_ref, lng_ref, lnb_ref, dw_ref, histc_ref, histd_ref,
     gf_ref, w1_ref, w2_ref, gfin_ref) = refs[n_x:n_x + 14]
    out_refs = refs[n_x + 14:n_x + 14 + n_out]
    (cp_ref, dp_ref, cs_ref, ds_ref,
     x1_ref, acc_ref, z_ref, xres_ref, h_ref, t_ref, extc_ref, extd_ref, carryc_ref, carryd_ref, conv_ref,
     ycat_ref, histc_scr, histd_scr) = refs[n_x + 14 + n_out:]
    step, is_prompt, j = _tile_ids(geo)
    tile, seg = geo.tile, geo.sample_len
    cwid = wout_ref.shape[0] // 2
    kc, kd = cw_ref.shape[0], dw_ref.shape[0]
    hc, hd = kc - 1, kd - 1
    pc, pd = carryc_ref.shape[0], carryd_ref.shape[0]
    o = 2 * cwid

    @pl.when(step == 0)
    def _():
        x1_ref[...] = jnp.zeros(x1_ref.shape, F32)
        carryc_ref[...] = jnp.zeros(carryc_ref.shape, F32)
        carryd_ref[...] = jnp.zeros(carryd_ref.shape, F32)

    ffn = _ffn_stages(x1_ref, gf_ref, w1_ref, w2_ref, acc_ref, h_ref, t_ref)
    _project(x_refs, is_prompt, g_ref, win_ref, xres_ref, z_ref)
    histc_scr[...] = histc_ref[...]
    histd_scr[...] = histd_ref[...]

    fresh = j == 0

    def conv_stage(s):
        def run(after=None):
            rows = slice(s * seg, (s + 1) * seg)
            c_in = z_ref[rows, 0:cwid] * jax.nn.sigmoid(z_ref[rows, cwid:2 * cwid])
            d_in = z_ref[rows, o + cwid:o + 2 * cwid] * z_ref[rows, o + 2 * cwid:o + 3 * cwid]
            if s == 0:
                prev_c = jnp.where(fresh, 0.0, carryc_ref[...])
                prev_d = jnp.where(fresh, 0.0, carryd_ref[...])
            else:
                prev_c = extc_ref[s - 1, seg:, :]
                prev_d = extd_ref[s - 1, seg:, :]
            extc_ref[s, 0:pc, :] = jnp.where(is_prompt, prev_c, histc_scr[s])
            extd_ref[s, 0:pd, :] = jnp.where(is_prompt, prev_d, histd_scr[s])
            extc_ref[s, pc:, :] = c_in
            extd_ref[s, pd:, :] = d_in
            extc, extd = extc_ref.at[s], extd_ref.at[s]
            if after is not None:
                _order_after(extc, after, range(cwid // LANES))
                _order_after(extd, after, range(cwid // LANES))
            cacc = None
            for lt in range(cwid // LANES):
                lanes = slice(lt * LANES, (lt + 1) * LANES)
                for r0 in range(0, seg, CONV_ROWS):
                    out_rows = slice(s * seg + r0, s * seg + r0 + CONV_ROWS)
                    cacc = _dwconv_rows(extc, pc - hc, cw_ref, kc, r0, CONV_ROWS, lanes)
                    conv_ref[out_rows, lanes] = cacc + cb_ref[:, lanes]
                    d = _dwconv_rows(extd, pd - hd, dw_ref, kd, r0, CONV_ROWS, lanes)
                    ycat_ref[out_rows, cwid + lt * LANES:cwid + (lt + 1) * LANES] = (
                        z_ref[out_rows, o + lt * LANES:o + (lt + 1) * LANES] * d).astype(BF16)
            return cacc[:SUBLANES]
        return run

    def norm_stage(b):
        def run():
            rows = slice(b * A_BLOCK, (b + 1) * A_BLOCK)
            c = _layer_norm(conv_ref[rows, :], lng_ref[...], lnb_ref[...])
            ycat_ref[rows, 0:cwid] = (c * jax.nn.sigmoid(c)).astype(BF16)
        return run

    assert len(ffn) == geo.seqs_per_tile
    tok_f, tok_c = None, None
    for k in range(len(ffn)):
        new_f = ffn[k](after=tok_c)
        tok_c = conv_stage(k)(after=tok_f)
        tok_f = new_f
    for b in range(tile // A_BLOCK):
        norm_stage(b)()
    last = geo.seqs_per_tile - 1
    carryc_ref[...] = extc_ref[last, seg:, :]
    carryd_ref[...] = extd_ref[last, seg:, :]

    y = jnp.dot(ycat_ref[...], wout_ref[...], preferred_element_type=F32)
    x1_ref[...] = xres_ref[...] + y

    tail_c, tail_d = slice(pc + seg - hc, pc + seg), slice(pd + seg - hd, pd + seg)
    for s in range(geo.seqs_per_tile):
        cs_ref[s] = extc_ref[s, tail_c, :]
        ds_ref[s] = extd_ref[s, tail_d, :]
    cp_ref[0] = extc_ref[last, tail_c, :]
    dp_ref[0] = extd_ref[last, tail_d, :]
    _emit_rows(geo, step, final, acc_ref, gfin_ref, out_refs)


def _layer_call(geo, name, body, xs, final, mixer_inputs, mixer_specs, ffn_inputs,
                mixer_out_shapes, mixer_out_specs, mixer_scratch, in_w):
    d, tile, pt = geo.d_model, geo.tile, geo.prompt_tiles
    g_ffn, w1, w2, g_final = ffn_inputs
    ffn_tile = geo.ffn_tile
    if len(xs) == 1:
        x_specs = [pl.BlockSpec((tile, d), lambda s: (geo.mix_tile(s), 0))]
    else:
        x_specs = [pl.BlockSpec((tile, d), lambda s: (jnp.minimum(s, pt - 1), 0)),
                   pl.BlockSpec((tile, d), lambda s: (jnp.clip(s - pt, 0, geo.sample_tiles - 1), 0))]
    ffn_specs = [_resident((1, d)), _resident((None,) + w1.shape[1:], (name[1], 0, 0)),
                 _resident((None,) + w2.shape[1:], (name[1], 0, 0)), _resident((1, d))]
    if final:
        row_shapes = [jax.ShapeDtypeStruct((pt * tile, d), F32),
                      jax.ShapeDtypeStruct((geo.sample_tiles * tile, d), F32)]
        row_specs = [pl.BlockSpec((tile, d), lambda s: (jnp.minimum(ffn_tile(s), pt - 1), 0)),
                     pl.BlockSpec((tile, d), lambda s: (jnp.maximum(ffn_tile(s) - pt, 0), 0))]
    else:
        row_shapes = [jax.ShapeDtypeStruct((geo.tiles * tile, d), F32)]
        row_specs = [pl.BlockSpec((tile, d), lambda s: (ffn_tile(s), 0))]
    scratch = [((tile, d), F32),
               ((tile, d), F32),
               ((tile, in_w), F32),
               ((tile, d), F32),
               ((tile, d), BF16),
               ((tile, FF_CHUNK), BF16)] + mixer_scratch
    mixer_io = list(zip(mixer_specs, mixer_inputs)) + list(zip(mixer_out_specs, mixer_out_shapes))
    block_bytes = ((len(xs) + len(row_shapes)) * _nbytes((tile, d), F32)
                   + sum(_spec_bytes(sp, a.dtype) for sp, a in mixer_io if sp.pipeline_mode is None))
    resident_bytes = (_nbytes(w1.shape[1:], w1.dtype) + _nbytes(w2.shape[1:], w2.dtype)
                      + sum(_spec_bytes(sp, a.dtype) for sp, a in mixer_io if sp.pipeline_mode is not None))
    scratch_bytes = sum(_nbytes(s, t) for s, t in scratch)
    temp_bytes = _nbytes((tile, FF_CHUNK), F32) + 2 * _nbytes((tile, d), F32)

    return pl.pallas_call(
        functools.partial(body, geo, len(xs), final),
        grid=(geo.steps,),
        in_specs=x_specs + mixer_specs + ffn_specs,
        out_specs=row_specs + mixer_out_specs,
        out_shape=row_shapes + mixer_out_shapes,
        scratch_shapes=[pltpu.VMEM(s, t) for s, t in scratch],
        compiler_params=pltpu.CompilerParams(
            dimension_semantics=("arbitrary",),
            vmem_limit_bytes=_vmem_limit(block_bytes, resident_bytes, scratch_bytes, temp_bytes)),
        name=f"{name[0]}_layer_{name[1]}",
    )(*xs, *mixer_inputs, g_ffn, w1, w2, g_final)


def _segment_maps(geo):
    pt = geo.prompt_tiles

    def sample_tile(s):
        return jnp.maximum(geo.mix_tile(s) - pt, 0)

    def prompt_seq(s):
        i = geo.mix_tile(s)
        return jnp.where(i < pt, i // geo.tiles_per_seq, geo.n_prompt_seq)

    return sample_tile, prompt_seq


def _even_layer(geo, layer, xs, final, g_mix, w_in, w_out, wmix, bmix, ln_g, ln_b, pool_w, b_scale,
                hist_pool, ffn_inputs):
    li = layer // 2
    tile, spt, seg, pt = geo.tile, geo.seqs_per_tile, geo.sample_len, geo.prompt_tiles
    in_w = w_in.shape[2]
    aw = w_out.shape[1] // 2
    bw = in_w - 2 * aw
    n_sample_seq = hist_pool.shape[1]
    sample_tile, prompt_seq = _segment_maps(geo)

    mixer_specs = [
        _resident((1, geo.d_model)),
        _resident((None,) + w_in.shape[1:], (li, 0, 0)),
        _resident((None,) + w_out.shape[1:], (li, 0, 0)),
        pl.BlockSpec((1,) + wmix.shape[1:], lambda s: (geo.mix_tile(s) // pt, 0, 0, 0)),
        pl.BlockSpec((1,) + bmix.shape[1:], lambda s: (geo.mix_tile(s) // pt, 0, 0)),
        _resident((1, aw)),
        _resident((1, aw)),
        _resident((None,) + pool_w.shape[1:], (li, 0, 0, 0)),
        _resident((1, bw)),
        pl.BlockSpec((None, spt, POOL_PAD, bw), lambda s: (li, sample_tile(s), 0, 0)),
    ]
    out_shapes = [
        jax.ShapeDtypeStruct((geo.sample_tiles * tile, aw), F32),
        jax.ShapeDtypeStruct((geo.n_prompt_seq + 1, POOL_HIST, bw), F32),
        jax.ShapeDtypeStruct((n_sample_seq, POOL_HIST, bw), F32),
    ]
    out_specs = [
        pl.BlockSpec((tile, aw), lambda s: (sample_tile(s), 0)),
        pl.BlockSpec((1, POOL_HIST, bw), lambda s: (prompt_seq(s), 0, 0)),
        pl.BlockSpec((spt, POOL_HIST, bw), lambda s: (sample_tile(s), 0, 0)),
    ]
    scratch = [
        ((spt, POOL_PAD + seg, bw), F32),
        ((POOL_PAD, bw), F32),
        ((tile, bw), F32),
        ((tile, aw + bw), BF16),
        ((spt, POOL_PAD, bw), F32),
        ((tile, aw), F32),
    ]
    outs = _layer_call(
        geo, ("even", layer), _even_layer_kernel, xs, final,
        (g_mix, w_in, w_out, wmix, bmix, ln_g, ln_b, pool_w, b_scale, hist_pool), mixer_specs, ffn_inputs,
        out_shapes, out_specs, scratch, in_w)
    n_rows = 2 if final else 1
    v_s, pool_p, pool_s = outs[n_rows:]
    return outs[:n_rows], v_s, pool_p[:geo.n_prompt_seq], pool_s


def _odd_layer(geo, layer, xs, final, g_mix, w_in, w_out, conv_w, conv_b, ln_g, ln_b, dconv_w,
               hist_c, hist_d, ffn_inputs):
    li = layer // 2
    tile, spt, seg = geo.tile, geo.seqs_per_tile, geo.sample_len
    in_w = w_in.shape[2]
    cwid = w_out.shape[1] // 2
    kc, kd = conv_w.shape[1], dconv_w.shape[1]
    hc, hd = kc - 1, kd - 1
    pc, pd = hist_c.shape[2], hist_d.shape[2]
    n_sample_seq = hist_c.shape[1]
    sample_tile, prompt_seq = _segment_maps(geo)

    mixer_specs = [
        _resident((1, geo.d_model)),
        _resident((None,) + w_in.shape[1:], (li, 0, 0)),
        _resident((None,) + w_out.shape[1:], (li, 0, 0)),
        _resident((None,) + conv_w.shape[1:], (li, 0, 0)),
        _resident((1, cwid)),
        _resident((1, cwid)),
        _resident((1, cwid)),
        _resident((None,) + dconv_w.shape[1:], (li, 0, 0)),
        pl.BlockSpec((None, spt, pc, cwid), lambda s: (li, sample_tile(s), 0, 0)),
        pl.BlockSpec((None, spt, pd, cwid), lambda s: (li, sample_tile(s), 0, 0)),
    ]
    out_shapes = [
        jax.ShapeDtypeStruct((geo.n_prompt_seq + 1, hc, cwid), F32),
        jax.ShapeDtypeStruct((geo.n_prompt_seq + 1, hd, cwid), F32),
        jax.ShapeDtypeStruct((n_sample_seq, hc, cwid), F32),
        jax.ShapeDtypeStruct((n_sample_seq, hd, cwid), F32),
    ]
    out_specs = [
        pl.BlockSpec((1, hc, cwid), lambda s: (prompt_seq(s), 0, 0)),
        pl.BlockSpec((1, hd, cwid), lambda s: (prompt_seq(s), 0, 0)),
        pl.BlockSpec((spt, hc, cwid), lambda s: (sample_tile(s), 0, 0)),
        pl.BlockSpec((spt, hd, cwid), lambda s: (sample_tile(s), 0, 0)),
    ]
    scratch = [
        ((spt, pc + seg, cwid), F32),
        ((spt, pd + seg, cwid), F32),
        ((pc, cwid), F32),
        ((pd, cwid), F32),
        ((tile, cwid), F32),
        ((tile, 2 * cwid), BF16),
        ((spt, pc, cwid), F32),
        ((spt, pd, cwid), F32),
    ]
    outs = _layer_call(
        geo, ("odd", layer), _odd_layer_kernel, xs, final,
        (g_mix, w_in, w_out, conv_w, conv_b, ln_g, ln_b, dconv_w, hist_c, hist_d), mixer_specs, ffn_inputs,
        out_shapes, out_specs, scratch, in_w)
    n_rows = 2 if final else 1
    c_p, d_p, c_s, d_s = outs[n_rows:]
    return outs[:n_rows], c_p[:geo.n_prompt_seq], d_p[:geo.n_prompt_seq], c_s, d_s


def _gating_operands(w_s, b_s, sample_len):
    heads, blk, _ = w_s.shape
    cidx = jnp.arange(blk) // CHUNK
    mask = (cidx[None, :] <= cidx[:, None]).astype(w_s.dtype)
    w_prompt = w_s * mask[None]
    head_dim_lanes = LANES // 2
    reps = blk // sample_len
    eye = jnp.eye(reps, dtype=w_s.dtype)
    w_first = w_prompt[:, :sample_len, :sample_len]
    w_sample = jnp.einsum("ab,hij->haibj", eye, w_first).reshape(heads, blk, blk)
    b_sample = jnp.tile(b_s[:, :sample_len], (1, reps))

    def pairs(w):
        return jnp.concatenate([w[0::2], w[1::2]], axis=2)

    def bias(b):
        return jnp.repeat(b.T, head_dim_lanes, axis=1)

    wmix = jnp.stack([pairs(w_prompt), pairs(w_sample)]).astype(BF16)
    bmix = jnp.stack([bias(b_s), bias(b_sample)]).astype(F32)
    return wmix, bmix


def _pad_history(state):
    rows = state.shape[2]
    return jnp.pad(state, ((0, 0), (0, 0), (_round_up(rows, SUBLANES) - rows, 0), (0, 0)))


def kernel(x_prompt, x_sample, state_pool, state_conv_c, state_conv_d, norm_mix_g, norm_ffn_g, final_norm_g,
           w_in_even, w_out_even, a_w_s, a_b_s, a_ln_g, a_ln_b, b_pool_w, b_scale,
           w_in_odd, w_out_odd, c_conv_w, c_conv_b, c_ln_g, c_ln_b, d_conv_w, w_ff1, w_ff2):
    n_prompt_seq, seq, d = x_prompt.shape
    n_sample_seq, sample_len, _ = x_sample.shape
    depth = norm_mix_g.shape[0]
    tile = ROW_TILE
    assert seq % tile == 0 and tile % A_BLOCK == 0 and tile % sample_len == 0
    assert A_BLOCK % sample_len == 0 and sample_len <= CHUNK
    assert (n_sample_seq * sample_len) % tile == 0
    assert a_w_s.shape[1] == A_HEADS and a_w_s.shape[2] == A_BLOCK
    assert w_out_even.shape[1] // 2 == A_HEADS * (LANES // 2)
    assert state_pool.shape[2] == POOL_HIST and sample_len >= _round_up(c_conv_w.shape[1] - 1, SUBLANES)
    geo = Geometry(
        d_model=d, tile=tile,
        prompt_tiles=n_prompt_seq * seq // tile, tiles_per_seq=seq // tile, n_prompt_seq=n_prompt_seq,
        sample_tiles=n_sample_seq * sample_len // tile, sample_len=sample_len,
        seqs_per_tile=tile // sample_len)

    def row(v):
        return v.reshape(1, -1)

    w_in_even_b, w_out_even_b = w_in_even.astype(BF16), w_out_even.astype(BF16)
    w_in_odd_b, w_out_odd_b = w_in_odd.astype(BF16), w_out_odd.astype(BF16)
    pool_w_b = b_pool_w.astype(BF16)
    w_ff1_b, w_ff2_b = w_ff1.astype(BF16), w_ff2.astype(BF16)
    hist_pool, hist_c, hist_d = _pad_history(state_pool), _pad_history(state_conv_c), _pad_history(state_conv_d)

    xs = (x_prompt.reshape(-1, d), x_sample.reshape(-1, d))
    pools_p, pools_s, vs, cs_p, cs_s, ds_p, ds_s = [], [], [], [], [], [], []
    for layer in range(depth):
        li = layer // 2
        final = layer == depth - 1
        ffn_inputs = (row(norm_ffn_g[layer]), w_ff1_b, w_ff2_b, row(final_norm_g))
        if layer % 2 == 0:
            wmix, bmix = _gating_operands(a_w_s[li], a_b_s[li], sample_len)
            xs, v_s, pool_p, pool_s = _even_layer(
                geo, layer, xs, final, row(norm_mix_g[layer]), w_in_even_b, w_out_even_b, wmix, bmix,
                row(a_ln_g[li]), row(a_ln_b[li]), pool_w_b, row(b_scale[li]), hist_pool, ffn_inputs)
            vs.append(v_s.reshape(n_sample_seq, sample_len, -1))
            pools_p.append(pool_p)
            pools_s.append(pool_s)
        else:
            xs, c_p, d_p, c_s, d_s = _odd_layer(
                geo, layer, xs, final, row(norm_mix_g[layer]), w_in_odd_b, w_out_odd_b, c_conv_w,
                row(c_conv_b[li]), row(c_ln_g[li]), row(c_ln_b[li]), d_conv_w, hist_c, hist_d, ffn_inputs)
            cs_p.append(c_p)
            ds_p.append(d_p)
            cs_s.append(c_s)
            ds_s.append(d_s)

    y_prompt = xs[0].reshape(n_prompt_seq, seq, d)
    y_sample = xs[1].reshape(n_sample_seq, sample_len, d)
    return (y_prompt, y_sample, jnp.stack(pools_p), jnp.stack(cs_p), jnp.stack(ds_p),
            jnp.stack(vs), jnp.stack(pools_s), jnp.stack(cs_s), jnp.stack(ds_s))
```

```python
import functools
from typing import NamedTuple

import jax
import jax.numpy as jnp
from jax import lax
from jax.experimental import pallas as pl
from jax.experimental.pallas import tpu as pltpu

F32 = jnp.float32
BF16 = jnp.bfloat16

EPS = 1e-6
PAST_LEN = 4096
CHUNK = 64
A_BLOCK = 128
A_HEADS = 8
POOL_WINDOWS = (2, 4, 8, 16)
LANES = 128
SUBLANES = 8
VMEM_CAPACITY_BYTES = 64 * 1024 * 1024
ROW_TILE = 512
FF_CHUNK = 1024
CONV_ROWS = 64

POOL_HIST = max(POOL_WINDOWS) - 1


def _round_up(n, m):
    return -(-n // m) * m


POOL_PAD = _round_up(POOL_HIST, SUBLANES)


class Geometry(NamedTuple):
    d_model: int
    tile: int
    prompt_tiles: int
    tiles_per_seq: int
    n_prompt_seq: int
    sample_tiles: int
    sample_len: int
    seqs_per_tile: int

    @property
    def tiles(self):
        return self.prompt_tiles + self.sample_tiles

    @property
    def steps(self):
        return self.tiles + 1

    def mix_tile(self, s):
        return jnp.minimum(s, self.tiles - 1)

    def ffn_tile(self, s):
        return jnp.maximum(s - 1, 0)


def _resident(shape, index=None):
    full = (0,) * len(shape) if index is None else tuple(index)
    return pl.BlockSpec(shape, lambda s: full, pipeline_mode=pl.Buffered(1))


def _spec_bytes(spec, dtype):
    return _nbytes([b for b in spec.block_shape if b is not None], dtype)


def _nbytes(shape, dtype):
    n = 1
    for s in shape:
        n *= s
    return n * jnp.dtype(dtype).itemsize


def _vmem_limit(block_bytes, resident_bytes, scratch_bytes, temp_bytes):
    need = 2 * block_bytes + resident_bytes + scratch_bytes + temp_bytes
    assert need <= VMEM_CAPACITY_BYTES, need
    return min(VMEM_CAPACITY_BYTES - (2 << 20), _round_up(need + need // 8, 1 << 20))


def _rms_norm(x, g):
    y = x * lax.rsqrt(jnp.mean(x * x, axis=-1, keepdims=True) + EPS)
    return y * g


def _layer_norm(x, g, b):
    mu = jnp.mean(x, axis=-1, keepdims=True)
    xc = x - mu
    y = xc * lax.rsqrt(jnp.mean(xc * xc, axis=-1, keepdims=True) + EPS)
    return y * g + b


def _gelu_tanh(x):
    c = 0.7978845608028654
    return x * (0.5 * (1.0 + jnp.tanh(c * (x + 0.044715 * (x * x * x)))))


def _load_rows(x_refs, is_prompt):
    if len(x_refs) == 1:
        return x_refs[0][...]
    return jnp.where(is_prompt, x_refs[0][...], x_refs[1][...])


def _tile_ids(geo):
    step = pl.program_id(0)
    m = geo.mix_tile(step)
    return step, m < geo.prompt_tiles, m % geo.tiles_per_seq


def _derived_zero(x):
    u = pltpu.bitcast(x, jnp.uint32)
    u = lax.shift_right_logical(lax.shift_right_logical(u, jnp.uint32(16)), jnp.uint32(16))
    return u.astype(F32)


def _order_after(ref, token, lane_tiles=(0,)):
    rows = SUBLANES * (4 // jnp.dtype(ref.dtype).itemsize)
    z = _derived_zero(token)
    z = jnp.concatenate([z] * (rows // SUBLANES), axis=0).astype(ref.dtype)
    for lt in lane_tiles:
        idx = (slice(0, rows), slice(lt * LANES, (lt + 1) * LANES))
        ref[idx] = ref[idx] + z


def _ffn_stages(x1_ref, g_ref, w1_ref, w2_ref, acc_ref, h_ref, t_ref):
    n_chunks = w1_ref.shape[1] // FF_CHUNK

    def up(c):
        def run(after=None):
            if c == 0:
                x = x1_ref[...]
                h_ref[...] = _rms_norm(x, g_ref[...]).astype(BF16)
                acc_ref[...] = x
            if after is not None:
                _order_after(h_ref, after)
            cols = slice(c * FF_CHUNK, (c + 1) * FF_CHUNK)
            t = jnp.maximum(jnp.dot(h_ref[...], w1_ref[:, cols], preferred_element_type=F32), 0.0)
            t_ref[...] = (t * t).astype(BF16)
            return t[-SUBLANES:, -LANES:]
        return run

    def down(c):
        def run(after=None):
            if after is not None:
                _order_after(t_ref, after)
            cols = slice(c * FF_CHUNK, (c + 1) * FF_CHUNK)
            d = jnp.dot(t_ref[...], w2_ref[cols, :], preferred_element_type=F32)
            acc_ref[...] += d
            return d[-SUBLANES:, -LANES:]
        return run

    return [stage(c) for c in range(n_chunks) for stage in (up, down)]


def _run_interleaved(*stage_lists):
    order = sorted(((k + 0.5) / len(stages), n, k) for n, stages in enumerate(stage_lists) for k in range(len(stages)))
    for _, n, k in order:
        stage_lists[n][k]()


def _project(x_refs, is_prompt, g_ref, win_ref, xres_ref, z_ref):
    x = _load_rows(x_refs, is_prompt)
    xres_ref[...] = x
    z_ref[...] = jnp.dot(_rms_norm(x, g_ref[...]).astype(BF16), win_ref[...], preferred_element_type=F32)


def _emit_rows(geo, step, final, acc_ref, gfin_ref, out_refs):
    if not final:
        out_refs[0][...] = acc_ref[...]
        return
    is_prompt = geo.ffn_tile(step) < geo.prompt_tiles

    @pl.when(is_prompt)
    def _():
        out_refs[0][...] = _rms_norm(acc_ref[...], gfin_ref[...])

    @pl.when(jnp.logical_not(is_prompt))
    def _():
        out_refs[1][...] = _rms_norm(acc_ref[...], gfin_ref[...])


def _pool_segment(ext, length, pos0, out_ref, out_base):
    pos = pos0 + lax.broadcasted_iota(jnp.int32, (length, 1), 0)
    for g, w in enumerate(POOL_WINDOWS):
        lanes = slice(g * LANES, (g + 1) * LANES)
        win = ext[0:POOL_PAD + length, lanes]
        s = win
        span = 1
        while span < w:
            s = s + pltpu.roll(s, span, axis=0)
            span *= 2
        inv_cnt = 1.0 / jnp.minimum(w, pos + 1).astype(F32)
        out_ref[out_base:out_base + length, lanes] = s[POOL_PAD:] * inv_cnt - win[POOL_PAD:]


def _even_layer_kernel(geo, n_x, final, *refs):
    n_out = 2 if final else 1
    x_refs = refs[:n_x]
    (g_ref, win_ref, wout_ref, wmix_ref, bmix_ref, lng_ref, lnb_ref, pw_ref, bscale_ref, hist_ref,
     gf_ref, w1_ref, w2_ref, gfin_ref) = refs[n_x:n_x + 14]
    out_refs = refs[n_x + 14:n_x + 14 + n_out]
    (v_ref, poolp_ref, pools_ref,
     x1_ref, acc_ref, z_ref, xres_ref, h_ref, t_ref, ext_ref, carry_ref, pooled_ref, ycat_ref,
     hist_scr, v_scr) = refs[n_x + 14 + n_out:]
    step, is_prompt, j = _tile_ids(geo)
    tile, seg = geo.tile, geo.sample_len
    aw = wout_ref.shape[0] // 2

    @pl.when(step == 0)
    def _():
        x1_ref[...] = jnp.zeros(x1_ref.shape, F32)
        carry_ref[...] = jnp.zeros(carry_ref.shape, F32)

    ffn = _ffn_stages(x1_ref, gf_ref, w1_ref, w2_ref, acc_ref, h_ref, t_ref)
    _project(x_refs, is_prompt, g_ref, win_ref, xres_ref, z_ref)
    hist_scr[...] = hist_ref[...]

    fresh = j == 0

    def pool_stage(s):
        def run():
            rows = slice(s * seg, (s + 1) * seg)
            if s == 0:
                before = jnp.where(fresh, 0.0, carry_ref[...])
            else:
                before = z_ref[s * seg - POOL_PAD:s * seg, 2 * aw:]
            ext_ref[s, 0:POOL_PAD, :] = jnp.where(is_prompt, before, hist_scr[s])
            ext_ref[s, POOL_PAD:, :] = z_ref[rows, 2 * aw:]
            pos0 = jnp.where(is_prompt, j * tile + s * seg, PAST_LEN)
            _pool_segment(ext_ref.at[s], seg, pos0, pooled_ref, s * seg)
        return run

    lane = lax.broadcasted_iota(jnp.int32, (A_BLOCK, LANES), 1)
    low_head = lane < (LANES // 2)
    n_pairs = aw // LANES

    def gate_stage(b):
        def run():
            rows = slice(b * A_BLOCK, (b + 1) * A_BLOCK)
            v = _layer_norm(_gelu_tanh(z_ref[rows, aw:2 * aw]), lng_ref[...], lnb_ref[...])
            v_scr[rows, :] = v
            mixed = []
            for p in range(n_pairs):
                vp = v[:, p * LANES:(p + 1) * LANES].astype(BF16)
                zero = jnp.zeros_like(vp)
                rhs = jnp.concatenate([jnp.where(low_head, vp, zero), jnp.where(low_head, zero, vp)], axis=0)
                mixed.append(jnp.dot(wmix_ref[0, p], rhs, preferred_element_type=F32))
            mixed = jnp.concatenate(mixed, axis=1) + bmix_ref[0]
            u = _gelu_tanh(z_ref[rows, 0:aw])
            ycat_ref[rows, 0:aw] = (u * mixed).astype(BF16)
        return run

    mixer = ([pool_stage(s) for s in range(geo.seqs_per_tile)]
             + [gate_stage(b) for b in range(tile // A_BLOCK)])
    _run_interleaved(mixer, ffn)
    carry_ref[...] = z_ref[tile - POOL_PAD:tile, 2 * aw:]

    for g in range(len(POOL_WINDOWS)):
        lanes = slice(g * LANES, (g + 1) * LANES)
        yb = jnp.dot(pooled_ref[:, lanes].astype(BF16), pw_ref[g], preferred_element_type=F32)
        ycat_ref[:, aw + g * LANES:aw + (g + 1) * LANES] = (yb * bscale_ref[:, lanes]).astype(BF16)

    y = jnp.dot(ycat_ref[...], wout_ref[...], preferred_element_type=F32)
    x1_ref[...] = xres_ref[...] + y

    tail = slice(POOL_PAD + seg - POOL_HIST, POOL_PAD + seg)
    v_ref[...] = v_scr[...]
    for s in range(geo.seqs_per_tile):
        pools_ref[s] = ext_ref[s, tail, :]
    poolp_ref[0] = ext_ref[geo.seqs_per_tile - 1, tail, :]
    _emit_rows(geo, step, final, acc_ref, gfin_ref, out_refs)


def _dwconv_rows(ext, first_row, w_ref, n_taps, r0, length, lanes, shift_ref):
    wlen = length + _round_up(first_row + n_taps - 1, SUBLANES)
    win = ext[r0:r0 + wlen, lanes]
    acc = None
    for r in range(SUBLANES):
        taps = [k for k in range(n_taps) if (first_row + k) % SUBLANES == r]
        if not taps:
            continue
        if r:
            shift_ref[r, SUBLANES - r:SUBLANES - r + wlen, :] = win
        for k in taps:
            q0 = (first_row + k) // SUBLANES * SUBLANES
            src = shift_ref[r, SUBLANES + q0:SUBLANES + q0 + length, :] if r else win[q0:q0 + length]
            term = src * w_ref[k:k + 1, lanes]
            acc = term if acc is None else acc + term
    return acc


def _odd_layer_kernel(geo, n_x, final, *refs):
    n_out = 2 if final else 1
    x_refs = refs[:n_x]
    (g_ref, win_ref, wout_ref, cw_ref, cb_ref, lng_ref, lnb_ref, dw_ref, histc_ref, histd_ref,
     gf_ref, w1_ref, w2_ref, gfin_ref) = refs[n_x:n_x + 14]
    out_refs = refs[n_x + 14:n_x + 14 + n_out]
    (cp_ref, dp_ref, cs_ref, ds_ref,
     x1_ref, acc_ref, z_ref, xres_ref, h_ref, t_ref, extc_ref, extd_ref, carryc_ref, carryd_ref, conv_ref,
     ycat_ref, histc_scr, histd_scr, shift_ref) = refs[n_x + 14 + n_out:]
    step, is_prompt, j = _tile_ids(geo)
    tile, seg = geo.tile, geo.sample_len
    cwid = wout_ref.shape[0] // 2
    kc, kd = cw_ref.shape[0], dw_ref.shape[0]
    hc, hd = kc - 1, kd - 1
    pc, pd = carryc_ref.shape[0], carryd_ref.shape[0]
    o = 2 * cwid

    @pl.when(step == 0)
    def _():
        x1_ref[...] = jnp.zeros(x1_ref.shape, F32)
        carryc_ref[...] = jnp.zeros(carryc_ref.shape, F32)
        carryd_ref[...] = jnp.zeros(carryd_ref.shape, F32)

    ffn = _ffn_stages(x1_ref, gf_ref, w1_ref, w2_ref, acc_ref, h_ref, t_ref)
    _project(x_refs, is_prompt, g_ref, win_ref, xres_ref, z_ref)
    histc_scr[...] = histc_ref[...]
    histd_scr[...] = histd_ref[...]

    fresh = j == 0

    def conv_stage(s):
        def run(after=None):
            rows = slice(s * seg, (s + 1) * seg)
            c_in = z_ref[rows, 0:cwid] * jax.nn.sigmoid(z_ref[rows, cwid:2 * cwid])
            d_in = z_ref[rows, o + cwid:o + 2 * cwid] * z_ref[rows, o + 2 * cwid:o + 3 * cwid]
            if s == 0:
                prev_c = jnp.where(fresh, 0.0, carryc_ref[...])
                prev_d = jnp.where(fresh, 0.0, carryd_ref[...])
            else:
                prev_c = extc_ref[s - 1, seg:, :]
                prev_d = extd_ref[s - 1, seg:, :]
            extc_ref[s, 0:pc, :] = jnp.where(is_prompt, prev_c, histc_scr[s])
            extd_ref[s, 0:pd, :] = jnp.where(is_prompt, prev_d, histd_scr[s])
            extc_ref[s, pc:, :] = c_in
            extd_ref[s, pd:, :] = d_in
            extc, extd = extc_ref.at[s], extd_ref.at[s]
            if after is not None:
                _order_after(extc, after, range(cwid // LANES))
                _order_after(extd, after, range(cwid // LANES))
            cacc = None
            for lt in range(cwid // LANES):
                lanes = slice(lt * LANES, (lt + 1) * LANES)
                for r0 in range(0, seg, CONV_ROWS):
                    out_rows = slice(s * seg + r0, s * seg + r0 + CONV_ROWS)
                    cacc = _dwconv_rows(extc, pc - hc, cw_ref, kc, r0, CONV_ROWS, lanes, shift_ref)
                    conv_ref[out_rows, lanes] = cacc + cb_ref[:, lanes]
                    d = _dwconv_rows(extd, pd - hd, dw_ref, kd, r0, CONV_ROWS, lanes, shift_ref)
                    ycat_ref[out_rows, cwid + lt * LANES:cwid + (lt + 1) * LANES] = (
                        z_ref[out_rows, o + lt * LANES:o + (lt + 1) * LANES] * d).astype(BF16)
            return cacc[:SUBLANES]
        return run

    def norm_stage(b):
        def run():
            rows = slice(b * A_BLOCK, (b + 1) * A_BLOCK)
            c = _layer_norm(conv_ref[rows, :], lng_ref[...], lnb_ref[...])
            ycat_ref[rows, 0:cwid] = (c * jax.nn.sigmoid(c)).astype(BF16)
        return run

    assert len(ffn) == geo.seqs_per_tile
    tok_f, tok_c = None, None
    for k in range(len(ffn)):
        new_f = ffn[k](after=tok_c)
        tok_c = conv_stage(k)(after=tok_f)
        tok_f = new_f
    for b in range(tile // A_BLOCK):
        norm_stage(b)()
    last = geo.seqs_per_tile - 1
    carryc_ref[...] = extc_ref[last, seg:, :]
    carryd_ref[...] = extd_ref[last, seg:, :]

    y = jnp.dot(ycat_ref[...], wout_ref[...], preferred_element_type=F32)
    x1_ref[...] = xres_ref[...] + y

    tail_c, tail_d = slice(pc + seg - hc, pc + seg), slice(pd + seg - hd, pd + seg)
    for s in range(geo.seqs_per_tile):
        cs_ref[s] = extc_ref[s, tail_c, :]
        ds_ref[s] = extd_ref[s, tail_d, :]
    cp_ref[0] = extc_ref[last, tail_c, :]
    dp_ref[0] = extd_ref[last, tail_d, :]
    _emit_rows(geo, step, final, acc_ref, gfin_ref, out_refs)


def _layer_call(geo, name, body, xs, final, mixer_inputs, mixer_specs, ffn_inputs,
                mixer_out_shapes, mixer_out_specs, mixer_scratch, in_w):
    d, tile, pt = geo.d_model, geo.tile, geo.prompt_tiles
    g_ffn, w1, w2, g_final = ffn_inputs
    ffn_tile = geo.ffn_tile
    if len(xs) == 1:
        x_specs = [pl.BlockSpec((tile, d), lambda s: (geo.mix_tile(s), 0))]
    else:
        x_specs = [pl.BlockSpec((tile, d), lambda s: (jnp.minimum(s, pt - 1), 0)),
                   pl.BlockSpec((tile, d), lambda s: (jnp.clip(s - pt, 0, geo.sample_tiles - 1), 0))]
    ffn_specs = [_resident((1, d)), _resident((None,) + w1.shape[1:], (name[1], 0, 0)),
                 _resident((None,) + w2.shape[1:], (name[1], 0, 0)), _resident((1, d))]
    if final:
        row_shapes = [jax.ShapeDtypeStruct((pt * tile, d), F32),
                      jax.ShapeDtypeStruct((geo.sample_tiles * tile, d), F32)]
        row_specs = [pl.BlockSpec((tile, d), lambda s: (jnp.minimum(ffn_tile(s), pt - 1), 0)),
                     pl.BlockSpec((tile, d), lambda s: (jnp.maximum(ffn_tile(s) - pt, 0), 0))]
    else:
        row_shapes = [jax.ShapeDtypeStruct((geo.tiles * tile, d), F32)]
        row_specs = [pl.BlockSpec((tile, d), lambda s: (ffn_tile(s), 0))]
    scratch = [((tile, d), F32),
               ((tile, d), F32),
               ((tile, in_w), F32),
               ((tile, d), F32),
               ((tile, d), BF16),
               ((tile, FF_CHUNK), BF16)] + mixer_scratch
    mixer_io = list(zip(mixer_specs, mixer_inputs)) + list(zip(mixer_out_specs, mixer_out_shapes))
    block_bytes = ((len(xs) + len(row_shapes)) * _nbytes((tile, d), F32)
                   + sum(_spec_bytes(sp, a.dtype) for sp, a in mixer_io if sp.pipeline_mode is None))
    resident_bytes = (_nbytes(w1.shape[1:], w1.dtype) + _nbytes(w2.shape[1:], w2.dtype)
                      + sum(_spec_bytes(sp, a.dtype) for sp, a in mixer_io if sp.pipeline_mode is not None))
    scratch_bytes = sum(_nbytes(s, t) for s, t in scratch)
    temp_bytes = _nbytes((tile, FF_CHUNK), F32) + 2 * _nbytes((tile, d), F32)

    return pl.pallas_call(
        functools.partial(body, geo, len(xs), final),
        grid=(geo.steps,),
        in_specs=x_specs + mixer_specs + ffn_specs,
        out_specs=row_specs + mixer_out_specs,
        out_shape=row_shapes + mixer_out_shapes,
        scratch_shapes=[pltpu.VMEM(s, t) for s, t in scratch],
        compiler_params=pltpu.CompilerParams(
            dimension_semantics=("arbitrary",),
            vmem_limit_bytes=_vmem_limit(block_bytes, resident_bytes, scratch_bytes, temp_bytes)),
        name=f"{name[0]}_layer_{name[1]}",
    )(*xs, *mixer_inputs, g_ffn, w1, w2, g_final)


def _segment_maps(geo):
    pt = geo.prompt_tiles

    def sample_tile(s):
        return jnp.maximum(geo.mix_tile(s) - pt, 0)

    def prompt_seq(s):
        i = geo.mix_tile(s)
        return jnp.where(i < pt, i // geo.tiles_per_seq, geo.n_prompt_seq)

    return sample_tile, prompt_seq


def _even_layer(geo, layer, xs, final, g_mix, w_in, w_out, wmix, bmix, ln_g, ln_b, pool_w, b_scale,
                hist_pool, ffn_inputs):
    li = layer // 2
    tile, spt, seg, pt = geo.tile, geo.seqs_per_tile, geo.sample_len, geo.prompt_tiles
    in_w = w_in.shape[2]
    aw = w_out.shape[1] // 2
    bw = in_w - 2 * aw
    n_sample_seq = hist_pool.shape[1]
    sample_tile, prompt_seq = _segment_maps(geo)

    mixer_specs = [
        _resident((1, geo.d_model)),
        _resident((None,) + w_in.shape[1:], (li, 0, 0)),
        _resident((None,) + w_out.shape[1:], (li, 0, 0)),
        pl.BlockSpec((1,) + wmix.shape[1:], lambda s: (geo.mix_tile(s) // pt, 0, 0, 0)),
        pl.BlockSpec((1,) + bmix.shape[1:], lambda s: (geo.mix_tile(s) // pt, 0, 0)),
        _resident((1, aw)),
        _resident((1, aw)),
        _resident((None,) + pool_w.shape[1:], (li, 0, 0, 0)),
        _resident((1, bw)),
        pl.BlockSpec((None, spt, POOL_PAD, bw), lambda s: (li, sample_tile(s), 0, 0)),
    ]
    out_shapes = [
        jax.ShapeDtypeStruct((geo.sample_tiles * tile, aw), F32),
        jax.ShapeDtypeStruct((geo.n_prompt_seq + 1, POOL_HIST, bw), F32),
        jax.ShapeDtypeStruct((n_sample_seq, POOL_HIST, bw), F32),
    ]
    out_specs = [
        pl.BlockSpec((tile, aw), lambda s: (sample_tile(s), 0)),
        pl.BlockSpec((1, POOL_HIST, bw), lambda s: (prompt_seq(s), 0, 0)),
        pl.BlockSpec((spt, POOL_HIST, bw), lambda s: (sample_tile(s), 0, 0)),
    ]
    scratch = [
        ((spt, POOL_PAD + seg, bw), F32),
        ((POOL_PAD, bw), F32),
        ((tile, bw), F32),
        ((tile, aw + bw), BF16),
        ((spt, POOL_PAD, bw), F32),
        ((tile, aw), F32),
    ]
    outs = _layer_call(
        geo, ("even", layer), _even_layer_kernel, xs, final,
        (g_mix, w_in, w_out, wmix, bmix, ln_g, ln_b, pool_w, b_scale, hist_pool), mixer_specs, ffn_inputs,
        out_shapes, out_specs, scratch, in_w)
    n_rows = 2 if final else 1
    v_s, pool_p, pool_s = outs[n_rows:]
    return outs[:n_rows], v_s, pool_p[:geo.n_prompt_seq], pool_s


def _odd_layer(geo, layer, xs, final, g_mix, w_in, w_out, conv_w, conv_b, ln_g, ln_b, dconv_w,
               hist_c, hist_d, ffn_inputs):
    li = layer // 2
    tile, spt, seg = geo.tile, geo.seqs_per_tile, geo.sample_len
    in_w = w_in.shape[2]
    cwid = w_out.shape[1] // 2
    kc, kd = conv_w.shape[1], dconv_w.shape[1]
    hc, hd = kc - 1, kd - 1
    pc, pd = hist_c.shape[2], hist_d.shape[2]
    n_sample_seq = hist_c.shape[1]
    sample_tile, prompt_seq = _segment_maps(geo)

    mixer_specs = [
        _resident((1, geo.d_model)),
        _resident((None,) + w_in.shape[1:], (li, 0, 0)),
        _resident((None,) + w_out.shape[1:], (li, 0, 0)),
        _resident((None,) + conv_w.shape[1:], (li, 0, 0)),
        _resident((1, cwid)),
        _resident((1, cwid)),
        _resident((1, cwid)),
        _resident((None,) + dconv_w.shape[1:], (li, 0, 0)),
        pl.BlockSpec((None, spt, pc, cwid), lambda s: (li, sample_tile(s), 0, 0)),
        pl.BlockSpec((None, spt, pd, cwid), lambda s: (li, sample_tile(s), 0, 0)),
    ]
    out_shapes = [
        jax.ShapeDtypeStruct((geo.n_prompt_seq + 1, hc, cwid), F32),
        jax.ShapeDtypeStruct((geo.n_prompt_seq + 1, hd, cwid), F32),
        jax.ShapeDtypeStruct((n_sample_seq, hc, cwid), F32),
        jax.ShapeDtypeStruct((n_sample_seq, hd, cwid), F32),
    ]
    out_specs = [
        pl.BlockSpec((1, hc, cwid), lambda s: (prompt_seq(s), 0, 0)),
        pl.BlockSpec((1, hd, cwid), lambda s: (prompt_seq(s), 0, 0)),
        pl.BlockSpec((spt, hc, cwid), lambda s: (sample_tile(s), 0, 0)),
        pl.BlockSpec((spt, hd, cwid), lambda s: (sample_tile(s), 0, 0)),
    ]
    scratch = [
        ((spt, pc + seg, cwid), F32),
        ((spt, pd + seg, cwid), F32),
        ((pc, cwid), F32),
        ((pd, cwid), F32),
        ((tile, cwid), F32),
        ((tile, 2 * cwid), BF16),
        ((spt, pc, cwid), F32),
        ((spt, pd, cwid), F32),
        ((SUBLANES, 2 * SUBLANES + pc + CONV_ROWS, LANES), F32),
    ]
    outs = _layer_call(
        geo, ("odd", layer), _odd_layer_kernel, xs, final,
        (g_mix, w_in, w_out, conv_w, conv_b, ln_g, ln_b, dconv_w, hist_c, hist_d), mixer_specs, ffn_inputs,
        out_shapes, out_specs, scratch, in_w)
    n_rows = 2 if final else 1
    c_p, d_p, c_s, d_s = outs[n_rows:]
    return outs[:n_rows], c_p[:geo.n_prompt_seq], d_p[:geo.n_prompt_seq], c_s, d_s


def _gating_operands(w_s, b_s, sample_len):
    heads, blk, _ = w_s.shape
    cidx = jnp.arange(blk) // CHUNK
    mask = (cidx[None, :] <= cidx[:, None]).astype(w_s.dtype)
    w_prompt = w_s * mask[None]
    head_dim_lanes = LANES // 2
    reps = blk // sample_len
    eye = jnp.eye(reps, dtype=w_s.dtype)
    w_first = w_prompt[:, :sample_len, :sample_len]
    w_sample = jnp.einsum("ab,hij->haibj", eye, w_first).reshape(heads, blk, blk)
    b_sample = jnp.tile(b_s[:, :sample_len], (1, reps))

    def pairs(w):
        return jnp.concatenate([w[0::2], w[1::2]], axis=2)

    def bias(b):
        return jnp.repeat(b.T, head_dim_lanes, axis=1)

    wmix = jnp.stack([pairs(w_prompt), pairs(w_sample)]).astype(BF16)
    bmix = jnp.stack([bias(b_s), bias(b_sample)]).astype(F32)
    return wmix, bmix


def _pad_history(state):
    rows = state.shape[2]
    return jnp.pad(state, ((0, 0), (0, 0), (_round_up(rows, SUBLANES) - rows, 0), (0, 0)))


def kernel(x_prompt, x_sample, state_pool, state_conv_c, state_conv_d, norm_mix_g, norm_ffn_g, final_norm_g,
           w_in_even, w_out_even, a_w_s, a_b_s, a_ln_g, a_ln_b, b_pool_w, b_scale,
           w_in_odd, w_out_odd, c_conv_w, c_conv_b, c_ln_g, c_ln_b, d_conv_w, w_ff1, w_ff2):
    n_prompt_seq, seq, d = x_prompt.shape
    n_sample_seq, sample_len, _ = x_sample.shape
    depth = norm_mix_g.shape[0]
    tile = ROW_TILE
    assert seq % tile == 0 and tile % A_BLOCK == 0 and tile % sample_len == 0
    assert A_BLOCK % sample_len == 0 and sample_len <= CHUNK
    assert (n_sample_seq * sample_len) % tile == 0
    assert a_w_s.shape[1] == A_HEADS and a_w_s.shape[2] == A_BLOCK
    assert w_out_even.shape[1] // 2 == A_HEADS * (LANES // 2)
    assert state_pool.shape[2] == POOL_HIST and sample_len >= _round_up(c_conv_w.shape[1] - 1, SUBLANES)
    geo = Geometry(
        d_model=d, tile=tile,
        prompt_tiles=n_prompt_seq * seq // tile, tiles_per_seq=seq // tile, n_prompt_seq=n_prompt_seq,
        sample_tiles=n_sample_seq * sample_len // tile, sample_len=sample_len,
        seqs_per_tile=tile // sample_len)

    def row(v):
        return v.reshape(1, -1)

    w_in_even_b, w_out_even_b = w_in_even.astype(BF16), w_out_even.astype(BF16)
    w_in_odd_b, w_out_odd_b = w_in_odd.astype(BF16), w_out_odd.astype(BF16)
    pool_w_b = b_pool_w.astype(BF16)
    w_ff1_b, w_ff2_b = w_ff1.astype(BF16), w_ff2.astype(BF16)
    hist_pool, hist_c, hist_d = _pad_history(state_pool), _pad_history(state_conv_c), _pad_history(state_conv_d)

    xs = (x_prompt.reshape(-1, d), x_sample.reshape(-1, d))
    pools_p, pools_s, vs, cs_p, cs_s, ds_p, ds_s = [], [], [], [], [], [], []
    for layer in range(depth):
        li = layer // 2
        final = layer == depth - 1
        ffn_inputs = (row(norm_ffn_g[layer]), w_ff1_b, w_ff2_b, row(final_norm_g))
        if layer % 2 == 0:
            wmix, bmix = _gating_operands(a_w_s[li], a_b_s[li], sample_len)
            xs, v_s, pool_p, pool_s = _even_layer(
                geo, layer, xs, final, row(norm_mix_g[layer]), w_in_even_b, w_out_even_b, wmix, bmix,
                row(a_ln_g[li]), row(a_ln_b[li]), pool_w_b, row(b_scale[li]), hist_pool, ffn_inputs)
            vs.append(v_s.reshape(n_sample_seq, sample_len, -1))
            pools_p.append(pool_p)
            pools_s.append(pool_s)
        else:
            xs, c_p, d_p, c_s, d_s = _odd_layer(
                geo, layer, xs, final, row(norm_mix_g[layer]), w_in_odd_b, w_out_odd_b, c_conv_w,
                row(c_conv_b[li]), row(c_ln_g[li]), row(c_ln_b[li]), d_conv_w, hist_c, hist_d, ffn_inputs)
            cs_p.append(c_p)
            ds_p.append(d_p)
            cs_s.append(c_s)
            ds_s.append(d_s)

    y_prompt = xs[0].reshape(n_prompt_seq, seq, d)
    y_sample = xs[1].reshape(n_sample_seq, sample_len, d)
    return (y_prompt, y_sample, jnp.stack(pools_p), jnp.stack(cs_p), jnp.stack(ds_p),
            jnp.stack(vs), jnp.stack(pools_s), jnp.stack(cs_s), jnp.stack(ds_s))
```

```python
import functools
from typing import NamedTuple

import jax
import jax.numpy as jnp
from jax import lax
from jax.experimental import pallas as pl
from jax.experimental.pallas import tpu as pltpu

F32 = jnp.float32
BF16 = jnp.bfloat16

EPS = 1e-6
PAST_LEN = 4096
CHUNK = 64
A_BLOCK = 128
A_HEADS = 8
POOL_WINDOWS = (2, 4, 8, 16)
LANES = 128
SUBLANES = 8
VMEM_CAPACITY_BYTES = 64 * 1024 * 1024
ROW_TILE = 512
FF_CHUNK = 1024
CONV_ROWS = 64
WEIGHT_STAGE_ROWS = 64

POOL_HIST = max(POOL_WINDOWS) - 1


def _round_up(n, m):
    return -(-n // m) * m


POOL_PAD = _round_up(POOL_HIST, SUBLANES)


class Geometry(NamedTuple):
    d_model: int
    tile: int
    prompt_tiles: int
    tiles_per_seq: int
    n_prompt_seq: int
    sample_tiles: int
    sample_len: int
    seqs_per_tile: int

    @property
    def tiles(self):
        return self.prompt_tiles + self.sample_tiles

    @property
    def steps(self):
        return self.tiles + 1

    def mix_tile(self, s):
        return jnp.minimum(s, self.tiles - 1)

    def ffn_tile(self, s):
        return jnp.maximum(s - 1, 0)


def _resident(shape, index=None):
    full = (0,) * len(shape) if index is None else tuple(index)
    return pl.BlockSpec(shape, lambda s: full, pipeline_mode=pl.Buffered(1))


def _in_hbm():
    return pl.BlockSpec(memory_space=pl.ANY)


def _spec_bytes(spec, dtype):
    if spec.block_shape is None:
        return 0
    return _nbytes([b for b in spec.block_shape if b is not None], dtype)


def _nbytes(shape, dtype):
    n = 1
    for s in shape:
        n *= s
    return n * jnp.dtype(dtype).itemsize


def _vmem_limit(block_bytes, resident_bytes, scratch_bytes, temp_bytes):
    need = 2 * block_bytes + resident_bytes + scratch_bytes + temp_bytes
    assert need <= VMEM_CAPACITY_BYTES, need
    return min(VMEM_CAPACITY_BYTES - (2 << 20), _round_up(need + need // 8, 1 << 20))


def _fetch_bf16(src_hbm, index, dst_ref, stage_ref, sem):
    n_rows, cols = dst_ref.shape
    rows = stage_ref.shape[1]
    n_chunks = n_rows // rows

    def copy(c):
        return pltpu.make_async_copy(src_hbm.at[index, pl.ds(c * rows, rows), :],
                                     stage_ref.at[c % 2, :, pl.ds(0, cols)], sem.at[c % 2])

    copy(0).start()
    for c in range(n_chunks):
        if c + 1 < n_chunks:
            copy(c + 1).start()
        copy(c).wait()
        dst_ref[c * rows:(c + 1) * rows, :] = stage_ref[c % 2, :, 0:cols].astype(BF16)


def _fetch_weights(w_index, hbm_refs, dst_refs, stage_ref, sem):
    mixer_index, ffn_index = w_index
    for src, dst, index in zip(hbm_refs, dst_refs, (mixer_index, mixer_index, ffn_index, ffn_index)):
        _fetch_bf16(src, index, dst, stage_ref, sem)


def _rms_norm(x, g):
    y = x * lax.rsqrt(jnp.mean(x * x, axis=-1, keepdims=True) + EPS)
    return y * g


def _layer_norm(x, g, b):
    mu = jnp.mean(x, axis=-1, keepdims=True)
    xc = x - mu
    y = xc * lax.rsqrt(jnp.mean(xc * xc, axis=-1, keepdims=True) + EPS)
    return y * g + b


def _gelu_tanh(x):
    c = 0.7978845608028654
    return x * (0.5 * (1.0 + jnp.tanh(c * (x + 0.044715 * (x * x * x)))))


def _load_rows(x_refs, is_prompt):
    if len(x_refs) == 1:
        return x_refs[0][...]
    return jnp.where(is_prompt, x_refs[0][...], x_refs[1][...])


def _tile_ids(geo):
    step = pl.program_id(0)
    m = geo.mix_tile(step)
    return step, m < geo.prompt_tiles, m % geo.tiles_per_seq


def _derived_zero(x):
    u = pltpu.bitcast(x, jnp.uint32)
    u = lax.shift_right_logical(lax.shift_right_logical(u, jnp.uint32(16)), jnp.uint32(16))
    return u.astype(F32)


def _order_after(ref, token, lane_tiles=(0,)):
    rows = SUBLANES * (4 // jnp.dtype(ref.dtype).itemsize)
    z = _derived_zero(token)
    z = jnp.concatenate([z] * (rows // SUBLANES), axis=0).astype(ref.dtype)
    for lt in lane_tiles:
        idx = (slice(0, rows), slice(lt * LANES, (lt + 1) * LANES))
        ref[idx] = ref[idx] + z


def _ffn_stages(x1_ref, g_ref, w1_ref, w2_ref, acc_ref, h_ref, t_ref):
    n_chunks = w1_ref.shape[1] // FF_CHUNK

    def up(c):
        def run(after=None):
            if c == 0:
                x = x1_ref[...]
                h_ref[...] = _rms_norm(x, g_ref[...]).astype(BF16)
                acc_ref[...] = x
            if after is not None:
                _order_after(h_ref, after)
            cols = slice(c * FF_CHUNK, (c + 1) * FF_CHUNK)
            t = jnp.maximum(jnp.dot(h_ref[...], w1_ref[:, cols], preferred_element_type=F32), 0.0)
            t_ref[...] = (t * t).astype(BF16)
            return t[-SUBLANES:, -LANES:]
        return run

    def down(c):
        def run(after=None):
            if after is not None:
                _order_after(t_ref, after)
            cols = slice(c * FF_CHUNK, (c + 1) * FF_CHUNK)
            d = jnp.dot(t_ref[...], w2_ref[cols, :], preferred_element_type=F32)
            acc_ref[...] += d
            return d[-SUBLANES:, -LANES:]
        return run

    return [stage(c) for c in range(n_chunks) for stage in (up, down)]


def _run_interleaved(*stage_lists):
    order = sorted(((k + 0.5) / len(stages), n, k) for n, stages in enumerate(stage_lists) for k in range(len(stages)))
    for _, n, k in order:
        stage_lists[n][k]()


def _project(x_refs, is_prompt, g_ref, win_ref, xres_ref, z_ref):
    x = _load_rows(x_refs, is_prompt)
    xres_ref[...] = x
    z_ref[...] = jnp.dot(_rms_norm(x, g_ref[...]).astype(BF16), win_ref[...], preferred_element_type=F32)


def _emit_rows(geo, step, final, acc_ref, gfin_ref, out_refs):
    if not final:
        out_refs[0][...] = acc_ref[...]
        return
    is_prompt = geo.ffn_tile(step) < geo.prompt_tiles

    @pl.when(is_prompt)
    def _():
        out_refs[0][...] = _rms_norm(acc_ref[...], gfin_ref[...])

    @pl.when(jnp.logical_not(is_prompt))
    def _():
        out_refs[1][...] = _rms_norm(acc_ref[...], gfin_ref[...])


def _pool_segment(ext, length, pos0, out_ref, out_base):
    pos = pos0 + lax.broadcasted_iota(jnp.int32, (length, 1), 0)
    for g, w in enumerate(POOL_WINDOWS):
        lanes = slice(g * LANES, (g + 1) * LANES)
        win = ext[0:POOL_PAD + length, lanes]
        s = win
        span = 1
        while span < w:
            s = s + pltpu.roll(s, span, axis=0)
            span *= 2
        inv_cnt = 1.0 / jnp.minimum(w, pos + 1).astype(F32)
        out_ref[out_base:out_base + length, lanes] = s[POOL_PAD:] * inv_cnt - win[POOL_PAD:]


def _even_layer_kernel(geo, n_x, final, w_index, *refs):
    n_out = 2 if final else 1
    x_refs = refs[:n_x]
    (g_ref, win_hbm, wout_hbm, wmix_ref, bmix_ref, lng_ref, lnb_ref, pw_ref, bscale_ref, hist_ref,
     gf_ref, w1_hbm, w2_hbm, gfin_ref) = refs[n_x:n_x + 14]
    out_refs = refs[n_x + 14:n_x + 14 + n_out]
    (v_ref, poolp_ref, pools_ref,
     x1_ref, acc_ref, z_ref, xres_ref, h_ref, t_ref, win_ref, wout_ref, w1_ref, w2_ref, stage_ref, sem,
     ext_ref, carry_ref, pooled_ref, ycat_ref, hist_scr, v_scr) = refs[n_x + 14 + n_out:]
    step, is_prompt, j = _tile_ids(geo)
    tile, seg = geo.tile, geo.sample_len
    aw = wout_ref.shape[0] // 2

    @pl.when(step == 0)
    def _():
        x1_ref[...] = jnp.zeros(x1_ref.shape, F32)
        carry_ref[...] = jnp.zeros(carry_ref.shape, F32)
        _fetch_weights(w_index, (win_hbm, wout_hbm, w1_hbm, w2_hbm), (win_ref, wout_ref, w1_ref, w2_ref),
                       stage_ref, sem)

    ffn = _ffn_stages(x1_ref, gf_ref, w1_ref, w2_ref, acc_ref, h_ref, t_ref)
    _project(x_refs, is_prompt, g_ref, win_ref, xres_ref, z_ref)
    hist_scr[...] = hist_ref[...]

    fresh = j == 0

    def pool_stage(s):
        def run():
            rows = slice(s * seg, (s + 1) * seg)
            if s == 0:
                before = jnp.where(fresh, 0.0, carry_ref[...])
            else:
                before = z_ref[s * seg - POOL_PAD:s * seg, 2 * aw:]
            ext_ref[s, 0:POOL_PAD, :] = jnp.where(is_prompt, before, hist_scr[s])
            ext_ref[s, POOL_PAD:, :] = z_ref[rows, 2 * aw:]
            pos0 = jnp.where(is_prompt, j * tile + s * seg, PAST_LEN)
            _pool_segment(ext_ref.at[s], seg, pos0, pooled_ref, s * seg)
        return run

    lane = lax.broadcasted_iota(jnp.int32, (A_BLOCK, LANES), 1)
    low_head = lane < (LANES // 2)
    n_pairs = aw // LANES

    def gate_stage(b):
        def run():
            rows = slice(b * A_BLOCK, (b + 1) * A_BLOCK)
            v = _layer_norm(_gelu_tanh(z_ref[rows, aw:2 * aw]), lng_ref[...], lnb_ref[...])
            v_scr[rows, :] = v
            mixed = []
            for p in range(n_pairs):
                vp = v[:, p * LANES:(p + 1) * LANES].astype(BF16)
                zero = jnp.zeros_like(vp)
                rhs = jnp.concatenate([jnp.where(low_head, vp, zero), jnp.where(low_head, zero, vp)], axis=0)
                mixed.append(jnp.dot(wmix_ref[0, p], rhs, preferred_element_type=F32))
            mixed = jnp.concatenate(mixed, axis=1) + bmix_ref[0]
            u = _gelu_tanh(z_ref[rows, 0:aw])
            ycat_ref[rows, 0:aw] = (u * mixed).astype(BF16)
        return run

    mixer = ([pool_stage(s) for s in range(geo.seqs_per_tile)]
             + [gate_stage(b) for b in range(tile // A_BLOCK)])
    _run_interleaved(mixer, ffn)
    carry_ref[...] = z_ref[tile - POOL_PAD:tile, 2 * aw:]

    for g in range(len(POOL_WINDOWS)):
        lanes = slice(g * LANES, (g + 1) * LANES)
        yb = jnp.dot(pooled_ref[:, lanes].astype(BF16), pw_ref[g], preferred_element_type=F32)
        ycat_ref[:, aw + g * LANES:aw + (g + 1) * LANES] = (yb * bscale_ref[:, lanes]).astype(BF16)

    y = jnp.dot(ycat_ref[...], wout_ref[...], preferred_element_type=F32)
    x1_ref[...] = xres_ref[...] + y

    tail = slice(POOL_PAD + seg - POOL_HIST, POOL_PAD + seg)
    v_ref[...] = v_scr[...]
    for s in range(geo.seqs_per_tile):
        pools_ref[s] = ext_ref[s, tail, :]
    poolp_ref[0] = ext_ref[geo.seqs_per_tile - 1, tail, :]
    _emit_rows(geo, step, final, acc_ref, gfin_ref, out_refs)


def _dwconv_rows(ext, first_row, w_ref, n_taps, r0, length, lanes):
    wlen = length + _round_up(first_row + n_taps - 1, SUBLANES)
    win = ext[r0:r0 + wlen, lanes]
    acc = None
    for r in range(SUBLANES):
        taps = [k for k in range(n_taps) if (first_row + k) % SUBLANES == r]
        if not taps:
            continue
        shifted = win if r == 0 else pltpu.roll(win, wlen - r, axis=0)
        for k in taps:
            q0 = (first_row + k) // SUBLANES * SUBLANES
            term = shifted[q0:q0 + length] * w_ref[k:k + 1, lanes]
            acc = term if acc is None else acc + term
    return acc


def _odd_layer_kernel(geo, n_x, final, w_index, *refs):
    n_out = 2 if final else 1
    x_refs = refs[:n_x]
    (g_ref, win_hbm, wout_hbm, cw_ref, cb_ref, lng_ref, lnb_ref, dw_ref, histc_ref, histd_ref,
     gf_ref, w1_hbm, w2_hbm, gfin_ref) = refs[n_x:n_x + 14]
    out_refs = refs[n_x + 14:n_x + 14 + n_out]
    (cp_ref, dp_ref, cs_ref, ds_ref,
     x1_ref, acc_ref, z_ref, xres_ref, h_ref, t_ref, win_ref, wout_ref, w1_ref, w2_ref, stage_ref, sem,
     extc_ref, extd_ref, carryc_ref, carryd_ref, conv_ref,
     ycat_ref, histc_scr, histd_scr) = refs[n_x + 14 + n_out:]
    step, is_prompt, j = _tile_ids(geo)
    tile, seg = geo.tile, geo.sample_len
    cwid = wout_ref.shape[0] // 2
    kc, kd = cw_ref.shape[0], dw_ref.shape[0]
    hc, hd = kc - 1, kd - 1
    pc, pd = carryc_ref.shape[0], carryd_ref.shape[0]
    o = 2 * cwid

    @pl.when(step == 0)
    def _():
        x1_ref[...] = jnp.zeros(x1_ref.shape, F32)
        carryc_ref[...] = jnp.zeros(carryc_ref.shape, F32)
        carryd_ref[...] = jnp.zeros(carryd_ref.shape, F32)
        _fetch_weights(w_index, (win_hbm, wout_hbm, w1_hbm, w2_hbm), (win_ref, wout_ref, w1_ref, w2_ref),
                       stage_ref, sem)

    ffn = _ffn_stages(x1_ref, gf_ref, w1_ref, w2_ref, acc_ref, h_ref, t_ref)
    _project(x_refs, is_prompt, g_ref, win_ref, xres_ref, z_ref)
    histc_scr[...] = histc_ref[...]
    histd_scr[...] = histd_ref[...]

    fresh = j == 0

    def conv_stage(s):
        def run(after=None):
            rows = slice(s * seg, (s + 1) * seg)
            c_in = z_ref[rows, 0:cwid] * jax.nn.sigmoid(z_ref[rows, cwid:2 * cwid])
            d_in = z_ref[rows, o + cwid:o + 2 * cwid] * z_ref[rows, o + 2 * cwid:o + 3 * cwid]
            if s == 0:
                prev_c = jnp.where(fresh, 0.0, carryc_ref[...])
                prev_d = jnp.where(fresh, 0.0, carryd_ref[...])
            else:
                prev_c = extc_ref[s - 1, seg:, :]
                prev_d = extd_ref[s - 1, seg:, :]
            extc_ref[s, 0:pc, :] = jnp.where(is_prompt, prev_c, histc_scr[s])
            extd_ref[s, 0:pd, :] = jnp.where(is_prompt, prev_d, histd_scr[s])
            extc_ref[s, pc:, :] = c_in
            extd_ref[s, pd:, :] = d_in
            extc, extd = extc_ref.at[s], extd_ref.at[s]
            if after is not None:
                _order_after(extc, after, range(cwid // LANES))
                _order_after(extd, after, range(cwid // LANES))
            cacc = None
            for lt in range(cwid // LANES):
                lanes = slice(lt * LANES, (lt + 1) * LANES)
                for r0 in range(0, seg, CONV_ROWS):
                    out_rows = slice(s * seg + r0, s * seg + r0 + CONV_ROWS)
                    cacc = _dwconv_rows(extc, pc - hc, cw_ref, kc, r0, CONV_ROWS, lanes)
                    conv_ref[out_rows, lanes] = cacc + cb_ref[:, lanes]
                    d = _dwconv_rows(extd, pd - hd, dw_ref, kd, r0, CONV_ROWS, lanes)
                    ycat_ref[out_rows, cwid + lt * LANES:cwid + (lt + 1) * LANES] = (
                        z_ref[out_rows, o + lt * LANES:o + (lt + 1) * LANES] * d).astype(BF16)
            return cacc[:SUBLANES]
        return run

    def norm_stage(b):
        def run():
            rows = slice(b * A_BLOCK, (b + 1) * A_BLOCK)
            c = _layer_norm(conv_ref[rows, :], lng_ref[...], lnb_ref[...])
            ycat_ref[rows, 0:cwid] = (c * jax.nn.sigmoid(c)).astype(BF16)
        return run

    assert len(ffn) == geo.seqs_per_tile
    tok_f, tok_c = None, None
    for k in range(len(ffn)):
        new_f = ffn[k](after=tok_c)
        tok_c = conv_stage(k)(after=tok_f)
        tok_f = new_f
    for b in range(tile // A_BLOCK):
        norm_stage(b)()
    last = geo.seqs_per_tile - 1
    carryc_ref[...] = extc_ref[last, seg:, :]
    carryd_ref[...] = extd_ref[last, seg:, :]

    y = jnp.dot(ycat_ref[...], wout_ref[...], preferred_element_type=F32)
    x1_ref[...] = xres_ref[...] + y

    tail_c, tail_d = slice(pc + seg - hc, pc + seg), slice(pd + seg - hd, pd + seg)
    for s in range(geo.seqs_per_tile):
        cs_ref[s] = extc_ref[s, tail_c, :]
        ds_ref[s] = extd_ref[s, tail_d, :]
    cp_ref[0] = extc_ref[last, tail_c, :]
    dp_ref[0] = extd_ref[last, tail_d, :]
    _emit_rows(geo, step, final, acc_ref, gfin_ref, out_refs)


def _layer_call(geo, name, body, xs, final, mixer_inputs, mixer_specs, ffn_inputs,
                mixer_out_shapes, mixer_out_specs, mixer_scratch, in_w):
    d, tile, pt = geo.d_model, geo.tile, geo.prompt_tiles
    g_ffn, w1, w2, g_final = ffn_inputs
    ffn_tile = geo.ffn_tile
    if len(xs) == 1:
        x_specs = [pl.BlockSpec((tile, d), lambda s: (geo.mix_tile(s), 0))]
    else:
        x_specs = [pl.BlockSpec((tile, d), lambda s: (jnp.minimum(s, pt - 1), 0)),
                   pl.BlockSpec((tile, d), lambda s: (jnp.clip(s - pt, 0, geo.sample_tiles - 1), 0))]
    ffn_specs = [_resident((1, d)), _in_hbm(), _in_hbm(), _resident((1, d))]
    w_in, w_out = mixer_inputs[1], mixer_inputs[2]
    if final:
        row_shapes = [jax.ShapeDtypeStruct((pt * tile, d), F32),
                      jax.ShapeDtypeStruct((geo.sample_tiles * tile, d), F32)]
        row_specs = [pl.BlockSpec((tile, d), lambda s: (jnp.minimum(ffn_tile(s), pt - 1), 0)),
                     pl.BlockSpec((tile, d), lambda s: (jnp.maximum(ffn_tile(s) - pt, 0), 0))]
    else:
        row_shapes = [jax.ShapeDtypeStruct((geo.tiles * tile, d), F32)]
        row_specs = [pl.BlockSpec((tile, d), lambda s: (ffn_tile(s), 0))]
    scratch = [((tile, d), F32),
               ((tile, d), F32),
               ((tile, in_w), F32),
               ((tile, d), F32),
               ((tile, d), BF16),
               ((tile, FF_CHUNK), BF16),
               (w_in.shape[1:], BF16), (w_out.shape[1:], BF16),
               (w1.shape[1:], BF16), (w2.shape[1:], BF16),
               ((2, WEIGHT_STAGE_ROWS, max(w.shape[2] for w in (w_in, w_out, w1, w2))), F32)]
    n_common = len(scratch)
    scratch = scratch + mixer_scratch
    mixer_io = list(zip(mixer_specs, mixer_inputs)) + list(zip(mixer_out_specs, mixer_out_shapes))
    block_bytes = ((len(xs) + len(row_shapes)) * _nbytes((tile, d), F32)
                   + sum(_spec_bytes(sp, a.dtype) for sp, a in mixer_io if sp.pipeline_mode is None))
    resident_bytes = sum(_spec_bytes(sp, a.dtype) for sp, a in mixer_io if sp.pipeline_mode is not None)
    scratch_bytes = sum(_nbytes(s, t) for s, t in scratch)
    temp_bytes = _nbytes((tile, FF_CHUNK), F32)

    return pl.pallas_call(
        functools.partial(body, geo, len(xs), final, (name[1] // 2, name[1])),
        grid=(geo.steps,),
        in_specs=x_specs + mixer_specs + ffn_specs,
        out_specs=row_specs + mixer_out_specs,
        out_shape=row_shapes + mixer_out_shapes,
        scratch_shapes=([pltpu.VMEM(s, t) for s, t in scratch[:n_common]] + [pltpu.SemaphoreType.DMA((2,))]
                        + [pltpu.VMEM(s, t) for s, t in scratch[n_common:]]),
        compiler_params=pltpu.CompilerParams(
            dimension_semantics=("arbitrary",),
            vmem_limit_bytes=_vmem_limit(block_bytes, resident_bytes, scratch_bytes, temp_bytes)),
        name=f"{name[0]}_layer_{name[1]}",
    )(*xs, *mixer_inputs, g_ffn, w1, w2, g_final)


def _segment_maps(geo):
    pt = geo.prompt_tiles

    def sample_tile(s):
        return jnp.maximum(geo.mix_tile(s) - pt, 0)

    def prompt_seq(s):
        i = geo.mix_tile(s)
        return jnp.where(i < pt, i // geo.tiles_per_seq, geo.n_prompt_seq)

    return sample_tile, prompt_seq


def _even_layer(geo, layer, xs, final, g_mix, w_in, w_out, wmix, bmix, ln_g, ln_b, pool_w, b_scale,
                hist_pool, ffn_inputs):
    li = layer // 2
    tile, spt, seg, pt = geo.tile, geo.seqs_per_tile, geo.sample_len, geo.prompt_tiles
    in_w = w_in.shape[2]
    aw = w_out.shape[1] // 2
    bw = in_w - 2 * aw
    n_sample_seq = hist_pool.shape[1]
    sample_tile, prompt_seq = _segment_maps(geo)

    mixer_specs = [
        _resident((1, geo.d_model)),
        _in_hbm(),
        _in_hbm(),
        pl.BlockSpec((1,) + wmix.shape[1:], lambda s: (geo.mix_tile(s) // pt, 0, 0, 0)),
        pl.BlockSpec((1,) + bmix.shape[1:], lambda s: (geo.mix_tile(s) // pt, 0, 0)),
        _resident((1, aw)),
        _resident((1, aw)),
        _resident((None,) + pool_w.shape[1:], (li, 0, 0, 0)),
        _resident((1, bw)),
        pl.BlockSpec((None, spt, POOL_PAD, bw), lambda s: (li, sample_tile(s), 0, 0)),
    ]
    out_shapes = [
        jax.ShapeDtypeStruct((geo.sample_tiles * tile, aw), F32),
        jax.ShapeDtypeStruct((geo.n_prompt_seq + 1, POOL_HIST, bw), F32),
        jax.ShapeDtypeStruct((n_sample_seq, POOL_HIST, bw), F32),
    ]
    out_specs = [
        pl.BlockSpec((tile, aw), lambda s: (sample_tile(s), 0)),
        pl.BlockSpec((1, POOL_HIST, bw), lambda s: (prompt_seq(s), 0, 0)),
        pl.BlockSpec((spt, POOL_HIST, bw), lambda s: (sample_tile(s), 0, 0)),
    ]
    scratch = [
        ((spt, POOL_PAD + seg, bw), F32),
        ((POOL_PAD, bw), F32),
        ((tile, bw), F32),
        ((tile, aw + bw), BF16),
        ((spt, POOL_PAD, bw), F32),
        ((tile, aw), F32),
    ]
    outs = _layer_call(
        geo, ("even", layer), _even_layer_kernel, xs, final,
        (g_mix, w_in, w_out, wmix, bmix, ln_g, ln_b, pool_w, b_scale, hist_pool), mixer_specs, ffn_inputs,
        out_shapes, out_specs, scratch, in_w)
    n_rows = 2 if final else 1
    v_s, pool_p, pool_s = outs[n_rows:]
    return outs[:n_rows], v_s, pool_p[:geo.n_prompt_seq], pool_s


def _odd_layer(geo, layer, xs, final, g_mix, w_in, w_out, conv_w, conv_b, ln_g, ln_b, dconv_w,
               hist_c, hist_d, ffn_inputs):
    li = layer // 2
    tile, spt, seg = geo.tile, geo.seqs_per_tile, geo.sample_len
    in_w = w_in.shape[2]
    cwid = w_out.shape[1] // 2
    kc, kd = conv_w.shape[1], dconv_w.shape[1]
    hc, hd = kc - 1, kd - 1
    pc, pd = hist_c.shape[2], hist_d.shape[2]
    n_sample_seq = hist_c.shape[1]
    sample_tile, prompt_seq = _segment_maps(geo)

    mixer_specs = [
        _resident((1, geo.d_model)),
        _in_hbm(),
        _in_hbm(),
        _resident((None,) + conv_w.shape[1:], (li, 0, 0)),
        _resident((1, cwid)),
        _resident((1, cwid)),
        _resident((1, cwid)),
        _resident((None,) + dconv_w.shape[1:], (li, 0, 0)),
        pl.BlockSpec((None, spt, pc, cwid), lambda s: (li, sample_tile(s), 0, 0)),
        pl.BlockSpec((None, spt, pd, cwid), lambda s: (li, sample_tile(s), 0, 0)),
    ]
    out_shapes = [
        jax.ShapeDtypeStruct((geo.n_prompt_seq + 1, hc, cwid), F32),
        jax.ShapeDtypeStruct((geo.n_prompt_seq + 1, hd, cwid), F32),
        jax.ShapeDtypeStruct((n_sample_seq, hc, cwid), F32),
        jax.ShapeDtypeStruct((n_sample_seq, hd, cwid), F32),
    ]
    out_specs = [
        pl.BlockSpec((1, hc, cwid), lambda s: (prompt_seq(s), 0, 0)),
        pl.BlockSpec((1, hd, cwid), lambda s: (prompt_seq(s), 0, 0)),
        pl.BlockSpec((spt, hc, cwid), lambda s: (sample_tile(s), 0, 0)),
        pl.BlockSpec((spt, hd, cwid), lambda s: (sample_tile(s), 0, 0)),
    ]
    scratch = [
        ((spt, pc + seg, cwid), F32),
        ((spt, pd + seg, cwid), F32),
        ((pc, cwid), F32),
        ((pd, cwid), F32),
        ((tile, cwid), F32),
        ((tile, 2 * cwid), BF16),
        ((spt, pc, cwid), F32),
        ((spt, pd, cwid), F32),
    ]
    outs = _layer_call(
        geo, ("odd", layer), _odd_layer_kernel, xs, final,
        (g_mix, w_in, w_out, conv_w, conv_b, ln_g, ln_b, dconv_w, hist_c, hist_d), mixer_specs, ffn_inputs,
        out_shapes, out_specs, scratch, in_w)
    n_rows = 2 if final else 1
    c_p, d_p, c_s, d_s = outs[n_rows:]
    return outs[:n_rows], c_p[:geo.n_prompt_seq], d_p[:geo.n_prompt_seq], c_s, d_s


def _gating_operands(w_s, b_s, sample_len):
    heads, blk, _ = w_s.shape
    cidx = jnp.arange(blk) // CHUNK
    mask = (cidx[None, :] <= cidx[:, None]).astype(w_s.dtype)
    w_prompt = w_s * mask[None]
    head_dim_lanes = LANES // 2
    reps = blk // sample_len
    eye = jnp.eye(reps, dtype=w_s.dtype)
    w_first = w_prompt[:, :sample_len, :sample_len]
    w_sample = jnp.einsum("ab,hij->haibj", eye, w_first).reshape(heads, blk, blk)
    b_sample = jnp.tile(b_s[:, :sample_len], (1, reps))

    def pairs(w):
        return jnp.concatenate([w[0::2], w[1::2]], axis=2)

    def bias(b):
        return jnp.repeat(b.T, head_dim_lanes, axis=1)

    wmix = jnp.stack([pairs(w_prompt), pairs(w_sample)]).astype(BF16)
    bmix = jnp.stack([bias(b_s), bias(b_sample)]).astype(F32)
    return wmix, bmix


def _pad_history(state):
    rows = state.shape[2]
    return jnp.pad(state, ((0, 0), (0, 0), (_round_up(rows, SUBLANES) - rows, 0), (0, 0)))


def kernel(x_prompt, x_sample, state_pool, state_conv_c, state_conv_d, norm_mix_g, norm_ffn_g, final_norm_g,
           w_in_even, w_out_even, a_w_s, a_b_s, a_ln_g, a_ln_b, b_pool_w, b_scale,
           w_in_odd, w_out_odd, c_conv_w, c_conv_b, c_ln_g, c_ln_b, d_conv_w, w_ff1, w_ff2):
    n_prompt_seq, seq, d = x_prompt.shape
    n_sample_seq, sample_len, _ = x_sample.shape
    depth = norm_mix_g.shape[0]
    tile = ROW_TILE
    assert seq % tile == 0 and tile % A_BLOCK == 0 and tile % sample_len == 0
    assert A_BLOCK % sample_len == 0 and sample_len <= CHUNK
    assert (n_sample_seq * sample_len) % tile == 0
    assert a_w_s.shape[1] == A_HEADS and a_w_s.shape[2] == A_BLOCK
    assert w_out_even.shape[1] // 2 == A_HEADS * (LANES // 2)
    assert state_pool.shape[2] == POOL_HIST and sample_len >= _round_up(c_conv_w.shape[1] - 1, SUBLANES)
    geo = Geometry(
        d_model=d, tile=tile,
        prompt_tiles=n_prompt_seq * seq // tile, tiles_per_seq=seq // tile, n_prompt_seq=n_prompt_seq,
        sample_tiles=n_sample_seq * sample_len // tile, sample_len=sample_len,
        seqs_per_tile=tile // sample_len)

    def row(v):
        return v.reshape(1, -1)

    pool_w_b = b_pool_w.astype(BF16)
    hist_pool, hist_c, hist_d = _pad_history(state_pool), _pad_history(state_conv_c), _pad_history(state_conv_d)

    xs = (x_prompt.reshape(-1, d), x_sample.reshape(-1, d))
    pools_p, pools_s, vs, cs_p, cs_s, ds_p, ds_s = [], [], [], [], [], [], []
    for layer in range(depth):
        li = layer // 2
        final = layer == depth - 1
        ffn_inputs = (row(norm_ffn_g[layer]), w_ff1, w_ff2, row(final_norm_g))
        if layer % 2 == 0:
            wmix, bmix = _gating_operands(a_w_s[li], a_b_s[li], sample_len)
            xs, v_s, pool_p, pool_s = _even_layer(
                geo, layer, xs, final, row(norm_mix_g[layer]), w_in_even, w_out_even, wmix, bmix,
                row(a_ln_g[li]), row(a_ln_b[li]), pool_w_b, row(b_scale[li]), hist_pool, ffn_inputs)
            vs.append(v_s.reshape(n_sample_seq, sample_len, -1))
            pools_p.append(pool_p)
            pools_s.append(pool_s)
        else:
            xs, c_p, d_p, c_s, d_s = _odd_layer(
                geo, layer, xs, final, row(norm_mix_g[layer]), w_in_odd, w_out_odd, c_conv_w,
                row(c_conv_b[li]), row(c_ln_g[li]), row(c_ln_b[li]), d_conv_w, hist_c, hist_d, ffn_inputs)
            cs_p.append(c_p)
            ds_p.append(d_p)
            cs_s.append(c_s)
            ds_s.append(d_s)

    y_prompt = xs[0].reshape(n_prompt_seq, seq, d)
    y_sample = xs[1].reshape(n_sample_seq, sample_len, d)
    return (y_prompt, y_sample, jnp.stack(pools_p), jnp.stack(cs_p), jnp.stack(ds_p),
            jnp.stack(vs), jnp.stack(pools_s), jnp.stack(cs_s), jnp.stack(ds_s))
```

```python
import functools
from typing import NamedTuple

import jax
import jax.numpy as jnp
from jax import lax
from jax.experimental import pallas as pl
from jax.experimental.pallas import tpu as pltpu

F32 = jnp.float32
BF16 = jnp.bfloat16

EPS = 1e-6
PAST_LEN = 4096
CHUNK = 64
A_BLOCK = 128
A_HEADS = 8
POOL_WINDOWS = (2, 4, 8, 16)
LANES = 128
SUBLANES = 8
VMEM_CAPACITY_BYTES = 64 * 1024 * 1024
ROW_TILE = 512
FF_CHUNK = 1024
CONV_ROWS = 64

POOL_HIST = max(POOL_WINDOWS) - 1


def _round_up(n, m):
    return -(-n // m) * m


POOL_PAD = _round_up(POOL_HIST, SUBLANES)


class Geometry(NamedTuple):
    d_model: int
    tile: int
    prompt_tiles: int
    tiles_per_seq: int
    n_prompt_seq: int
    sample_tiles: int
    sample_len: int
    seqs_per_tile: int

    @property
    def tiles(self):
        return self.prompt_tiles + self.sample_tiles

    @property
    def steps(self):
        return self.tiles + 1

    def mix_tile(self, s):
        return jnp.minimum(s, self.tiles - 1)

    def ffn_tile(self, s):
        return jnp.maximum(s - 1, 0)


def _resident(shape, index=None):
    full = (0,) * len(shape) if index is None else tuple(index)
    return pl.BlockSpec(shape, lambda s: full, pipeline_mode=pl.Buffered(1))


def _spec_bytes(spec, dtype):
    return _nbytes([b for b in spec.block_shape if b is not None], dtype)


def _nbytes(shape, dtype):
    n = 1
    for s in shape:
        n *= s
    return n * jnp.dtype(dtype).itemsize


def _vmem_limit(block_bytes, resident_bytes, scratch_bytes, temp_bytes):
    need = 2 * block_bytes + resident_bytes + scratch_bytes + temp_bytes
    assert need <= VMEM_CAPACITY_BYTES, need
    return min(VMEM_CAPACITY_BYTES - (2 << 20), _round_up(need + need // 8, 1 << 20))


def _rms_norm(x, g):
    y = x * lax.rsqrt(jnp.mean(x * x, axis=-1, keepdims=True) + EPS)
    return y * g


def _layer_norm(x, g, b):
    mu = jnp.mean(x, axis=-1, keepdims=True)
    xc = x - mu
    y = xc * lax.rsqrt(jnp.mean(xc * xc, axis=-1, keepdims=True) + EPS)
    return y * g + b


def _gelu_tanh(x):
    c = 0.7978845608028654
    return x * (0.5 * (1.0 + jnp.tanh(c * (x + 0.044715 * (x * x * x)))))


def _load_rows(x_refs, is_prompt):
    if len(x_refs) == 1:
        return x_refs[0][...]
    return jnp.where(is_prompt, x_refs[0][...], x_refs[1][...])


def _tile_ids(geo):
    step = pl.program_id(0)
    m = geo.mix_tile(step)
    return step, m < geo.prompt_tiles, m % geo.tiles_per_seq


def _derived_zero(x):
    u = pltpu.bitcast(x, jnp.uint32)
    u = lax.shift_right_logical(lax.shift_right_logical(u, jnp.uint32(16)), jnp.uint32(16))
    return u.astype(F32)


def _order_after(ref, token, lane_tiles=(0,)):
    rows = SUBLANES * (4 // jnp.dtype(ref.dtype).itemsize)
    z = _derived_zero(token)
    z = jnp.concatenate([z] * (rows // SUBLANES), axis=0).astype(ref.dtype)
    for lt in lane_tiles:
        idx = (slice(0, rows), slice(lt * LANES, (lt + 1) * LANES))
        ref[idx] = ref[idx] + z


def _ffn_stages(x1_ref, g_ref, w1_ref, w2_ref, acc_ref, h_ref, t_ref, dst_ref=None):
    n_chunks = w1_ref.shape[1] // FF_CHUNK

    def up(c):
        def run(after=None):
            if c == 0:
                x = x1_ref[...]
                h_ref[...] = _rms_norm(x, g_ref[...]).astype(BF16)
                acc_ref[...] = x
            if after is not None:
                _order_after(h_ref, after)
            cols = slice(c * FF_CHUNK, (c + 1) * FF_CHUNK)
            t = jnp.maximum(jnp.dot(h_ref[...], w1_ref[:, cols], preferred_element_type=F32), 0.0)
            t_ref[...] = (t * t).astype(BF16)
            return t[-SUBLANES:, -LANES:]
        return run

    def down(c):
        def run(after=None):
            if after is not None:
                _order_after(t_ref, after)
            cols = slice(c * FF_CHUNK, (c + 1) * FF_CHUNK)
            d = jnp.dot(t_ref[...], w2_ref[cols, :], preferred_element_type=F32)
            if dst_ref is not None and c == n_chunks - 1:
                dst_ref[...] = acc_ref[...] + d
            else:
                acc_ref[...] += d
            return d[-SUBLANES:, -LANES:]
        return run

    return [stage(c) for c in range(n_chunks) for stage in (up, down)]


def _run_interleaved(*stage_lists):
    order = sorted(((k + 0.5) / len(stages), n, k) for n, stages in enumerate(stage_lists) for k in range(len(stages)))
    for _, n, k in order:
        stage_lists[n][k]()


def _project(x_refs, is_prompt, g_ref, win_ref, xres_ref, z_ref):
    x = _load_rows(x_refs, is_prompt)
    xres_ref[...] = x
    z_ref[...] = jnp.dot(_rms_norm(x, g_ref[...]).astype(BF16), win_ref[...], preferred_element_type=F32)


def _emit_rows(geo, step, final, acc_ref, gfin_ref, out_refs):
    if not final:
        return
    is_prompt = geo.ffn_tile(step) < geo.prompt_tiles

    @pl.when(is_prompt)
    def _():
        out_refs[0][...] = _rms_norm(acc_ref[...], gfin_ref[...])

    @pl.when(jnp.logical_not(is_prompt))
    def _():
        out_refs[1][...] = _rms_norm(acc_ref[...], gfin_ref[...])


def _pool_segment(ext, length, pos0, out_ref, out_base):
    pos = pos0 + lax.broadcasted_iota(jnp.int32, (length, 1), 0)
    for g, w in enumerate(POOL_WINDOWS):
        lanes = slice(g * LANES, (g + 1) * LANES)
        win = ext[0:POOL_PAD + length, lanes]
        s = win
        span = 1
        while span < w:
            s = s + pltpu.roll(s, span, axis=0)
            span *= 2
        inv_cnt = 1.0 / jnp.minimum(w, pos + 1).astype(F32)
        out_ref[out_base:out_base + length, lanes] = s[POOL_PAD:] * inv_cnt - win[POOL_PAD:]


def _even_layer_kernel(geo, n_x, final, *refs):
    n_out = 2 if final else 1
    x_refs = refs[:n_x]
    (g_ref, win_ref, wout_ref, wmix_ref, bmix_ref, lng_ref, lnb_ref, pw_ref, bscale_ref, hist_ref,
     gf_ref, w1_ref, w2_ref, gfin_ref) = refs[n_x:n_x + 14]
    out_refs = refs[n_x + 14:n_x + 14 + n_out]
    (v_ref, poolp_ref, pools_ref,
     x1_ref, acc_ref, z_ref, xres_ref, h_ref, t_ref, ext_ref, carry_ref, pooled_ref,
     ycat_ref) = refs[n_x + 14 + n_out:]
    step, is_prompt, j = _tile_ids(geo)
    tile, seg = geo.tile, geo.sample_len
    aw = wout_ref.shape[0] // 2

    @pl.when(step == 0)
    def _():
        x1_ref[...] = jnp.zeros(x1_ref.shape, F32)
        carry_ref[...] = jnp.zeros(carry_ref.shape, F32)

    ffn = _ffn_stages(x1_ref, gf_ref, w1_ref, w2_ref, acc_ref, h_ref, t_ref, None if final else out_refs[0])
    _project(x_refs, is_prompt, g_ref, win_ref, xres_ref, z_ref)

    fresh = j == 0

    def pool_stage(s):
        def run():
            rows = slice(s * seg, (s + 1) * seg)
            if s == 0:
                before = jnp.where(fresh, 0.0, carry_ref[...])
            else:
                before = z_ref[s * seg - POOL_PAD:s * seg, 2 * aw:]
            ext_ref[s, 0:POOL_PAD, :] = jnp.where(is_prompt, before, hist_ref[s])
            ext_ref[s, POOL_PAD:, :] = z_ref[rows, 2 * aw:]
            pos0 = jnp.where(is_prompt, j * tile + s * seg, PAST_LEN)
            _pool_segment(ext_ref.at[s], seg, pos0, pooled_ref, s * seg)
        return run

    lane = lax.broadcasted_iota(jnp.int32, (A_BLOCK, LANES), 1)
    low_head = lane < (LANES // 2)
    n_pairs = aw // LANES

    def gate_stage(b):
        def run():
            blocks = (slice(b * A_BLOCK, (b + 1) * A_BLOCK), slice((b + 1) * A_BLOCK, (b + 2) * A_BLOCK))
            vs = []
            for rows in blocks:
                v = _layer_norm(_gelu_tanh(z_ref[rows, aw:2 * aw]), lng_ref[...], lnb_ref[...])
                v_ref[rows, :] = v
                vs.append(v)
            mixed = ([], [])
            for p in range(n_pairs):
                rhs = []
                for v in vs:
                    vp = v[:, p * LANES:(p + 1) * LANES].astype(BF16)
                    zero = jnp.zeros_like(vp)
                    rhs.append(jnp.concatenate([jnp.where(low_head, vp, zero), jnp.where(low_head, zero, vp)],
                                               axis=0))
                both = jnp.dot(wmix_ref[0, p], jnp.concatenate(rhs, axis=1), preferred_element_type=F32)
                mixed[0].append(both[:, :LANES])
                mixed[1].append(both[:, LANES:])
            for rows, parts in zip(blocks, mixed):
                u = _gelu_tanh(z_ref[rows, 0:aw])
                ycat_ref[rows, 0:aw] = (u * (jnp.concatenate(parts, axis=1) + bmix_ref[0])).astype(BF16)
        return run

    mixer = ([pool_stage(s) for s in range(geo.seqs_per_tile)]
             + [gate_stage(b) for b in range(0, tile // A_BLOCK, 2)])
    _run_interleaved(mixer, ffn)
    carry_ref[...] = z_ref[tile - POOL_PAD:tile, 2 * aw:]

    for g in range(len(POOL_WINDOWS)):
        lanes = slice(g * LANES, (g + 1) * LANES)
        yb = jnp.dot(pooled_ref[:, lanes].astype(BF16), pw_ref[g], preferred_element_type=F32)
        ycat_ref[:, aw + g * LANES:aw + (g + 1) * LANES] = (yb * bscale_ref[:, lanes]).astype(BF16)

    y = jnp.dot(ycat_ref[...], wout_ref[...], preferred_element_type=F32)
    x1_ref[...] = xres_ref[...] + y

    tail = slice(POOL_PAD + seg - POOL_HIST, POOL_PAD + seg)
    for s in range(geo.seqs_per_tile):
        pools_ref[s] = ext_ref[s, tail, :]
    poolp_ref[0] = ext_ref[geo.seqs_per_tile - 1, tail, :]
    _emit_rows(geo, step, final, acc_ref, gfin_ref, out_refs)


def _dwconv_rows(ext, first_row, w_ref, n_taps, r0, length, lanes):
    wlen = length + _round_up(first_row + n_taps - 1, SUBLANES)
    win = ext[r0:r0 + wlen, lanes]
    acc = None
    for r in range(SUBLANES):
        taps = [k for k in range(n_taps) if (first_row + k) % SUBLANES == r]
        if not taps:
            continue
        shifted = win if r == 0 else pltpu.roll(win, wlen - r, axis=0)
        for k in taps:
            q0 = (first_row + k) // SUBLANES * SUBLANES
            term = shifted[q0:q0 + length] * w_ref[k:k + 1, lanes]
            acc = term if acc is None else acc + term
    return acc


def _odd_layer_kernel(geo, n_x, final, *refs):
    n_out = 2 if final else 1
    x_refs = refs[:n_x]
    (g_ref, win_ref, wout_ref, cw_ref, cb_ref, lng_ref, lnb_ref, dw_ref, histc_ref, histd_ref,
     gf_ref, w1_ref, w2_ref, gfin_ref) = refs[n_x:n_x + 14]
    out_refs = refs[n_x + 14:n_x + 14 + n_out]
    (cp_ref, dp_ref, cs_ref, ds_ref,
     x1_ref, acc_ref, z_ref, xres_ref, h_ref, t_ref, extc_ref, extd_ref, carryc_ref, carryd_ref, conv_ref,
     ycat_ref) = refs[n_x + 14 + n_out:]
    step, is_prompt, j = _tile_ids(geo)
    tile, seg = geo.tile, geo.sample_len
    cwid = wout_ref.shape[0] // 2
    kc, kd = cw_ref.shape[0], dw_ref.shape[0]
    hc, hd = kc - 1, kd - 1
    pc, pd = carryc_ref.shape[0], carryd_ref.shape[0]
    o = 2 * cwid

    @pl.when(step == 0)
    def _():
        x1_ref[...] = jnp.zeros(x1_ref.shape, F32)
        carryc_ref[...] = jnp.zeros(carryc_ref.shape, F32)
        carryd_ref[...] = jnp.zeros(carryd_ref.shape, F32)

    ffn = _ffn_stages(x1_ref, gf_ref, w1_ref, w2_ref, acc_ref, h_ref, t_ref, None if final else out_refs[0])
    _project(x_refs, is_prompt, g_ref, win_ref, xres_ref, z_ref)

    fresh = j == 0

    def conv_stage(s):
        def run(after=None):
            rows = slice(s * seg, (s + 1) * seg)
            c_in = z_ref[rows, 0:cwid] * jax.nn.sigmoid(z_ref[rows, cwid:2 * cwid])
            d_in = z_ref[rows, o + cwid:o + 2 * cwid] * z_ref[rows, o + 2 * cwid:o + 3 * cwid]
            if s == 0:
                prev_c = jnp.where(fresh, 0.0, carryc_ref[...])
                prev_d = jnp.where(fresh, 0.0, carryd_ref[...])
            else:
                prev_c = extc_ref[s - 1, seg:, :]
                prev_d = extd_ref[s - 1, seg:, :]
            extc_ref[s, 0:pc, :] = jnp.where(is_prompt, prev_c, histc_ref[s])
            extd_ref[s, 0:pd, :] = jnp.where(is_prompt, prev_d, histd_ref[s])
            extc_ref[s, pc:, :] = c_in
            extd_ref[s, pd:, :] = d_in
            extc, extd = extc_ref.at[s], extd_ref.at[s]
            if after is not None:
                _order_after(extc, after, range(cwid // LANES))
                _order_after(extd, after, range(cwid // LANES))
            cacc = None
            for lt in range(cwid // LANES):
                lanes = slice(lt * LANES, (lt + 1) * LANES)
                for r0 in range(0, seg, CONV_ROWS):
                    out_rows = slice(s * seg + r0, s * seg + r0 + CONV_ROWS)
                    cacc = _dwconv_rows(extc, pc - hc, cw_ref, kc, r0, CONV_ROWS, lanes)
                    conv_ref[out_rows, lanes] = cacc + cb_ref[:, lanes]
                    d = _dwconv_rows(extd, pd - hd, dw_ref, kd, r0, CONV_ROWS, lanes)
                    ycat_ref[out_rows, cwid + lt * LANES:cwid + (lt + 1) * LANES] = (
                        z_ref[out_rows, o + lt * LANES:o + (lt + 1) * LANES] * d).astype(BF16)
            return cacc[:SUBLANES]
        return run

    def norm_stage(b):
        def run():
            rows = slice(b * A_BLOCK, (b + 1) * A_BLOCK)
            c = _layer_norm(conv_ref[rows, :], lng_ref[...], lnb_ref[...])
            ycat_ref[rows, 0:cwid] = (c * jax.nn.sigmoid(c)).astype(BF16)
        return run

    assert len(ffn) == geo.seqs_per_tile
    tok_f, tok_c = None, None
    for k in range(len(ffn)):
        new_f = ffn[k](after=tok_c)
        tok_c = conv_stage(k)(after=tok_f)
        tok_f = new_f
    for b in range(tile // A_BLOCK):
        norm_stage(b)()
    last = geo.seqs_per_tile - 1
    carryc_ref[...] = extc_ref[last, seg:, :]
    carryd_ref[...] = extd_ref[last, seg:, :]

    y = jnp.dot(ycat_ref[...], wout_ref[...], preferred_element_type=F32)
    x1_ref[...] = xres_ref[...] + y

    tail_c, tail_d = slice(pc + seg - hc, pc + seg), slice(pd + seg - hd, pd + seg)
    for s in range(geo.seqs_per_tile):
        cs_ref[s] = extc_ref[s, tail_c, :]
        ds_ref[s] = extd_ref[s, tail_d, :]
    cp_ref[0] = extc_ref[last, tail_c, :]
    dp_ref[0] = extd_ref[last, tail_d, :]
    _emit_rows(geo, step, final, acc_ref, gfin_ref, out_refs)


def _layer_call(geo, name, body, xs, final, mixer_inputs, mixer_specs, ffn_inputs,
                mixer_out_shapes, mixer_out_specs, mixer_scratch, in_w):
    d, tile, pt = geo.d_model, geo.tile, geo.prompt_tiles
    g_ffn, w1, w2, g_final = ffn_inputs
    ffn_tile = geo.ffn_tile
    if len(xs) == 1:
        x_specs = [pl.BlockSpec((tile, d), lambda s: (geo.mix_tile(s), 0))]
    else:
        x_specs = [pl.BlockSpec((tile, d), lambda s: (jnp.minimum(s, pt - 1), 0)),
                   pl.BlockSpec((tile, d), lambda s: (jnp.clip(s - pt, 0, geo.sample_tiles - 1), 0))]
    ffn_specs = [_resident((1, d)), _resident((None,) + w1.shape[1:], (name[1], 0, 0)),
                 _resident((None,) + w2.shape[1:], (name[1], 0, 0)), _resident((1, d))]
    if final:
        row_shapes = [jax.ShapeDtypeStruct((pt * tile, d), F32),
                      jax.ShapeDtypeStruct((geo.sample_tiles * tile, d), F32)]
        row_specs = [pl.BlockSpec((tile, d), lambda s: (jnp.minimum(ffn_tile(s), pt - 1), 0)),
                     pl.BlockSpec((tile, d), lambda s: (jnp.maximum(ffn_tile(s) - pt, 0), 0))]
    else:
        row_shapes = [jax.ShapeDtypeStruct((geo.tiles * tile, d), F32)]
        row_specs = [pl.BlockSpec((tile, d), lambda s: (ffn_tile(s), 0))]
    scratch = [((tile, d), F32),
               ((tile, d), F32),
               ((tile, in_w), F32),
               ((tile, d), F32),
               ((tile, d), BF16),
               ((tile, FF_CHUNK), BF16)] + mixer_scratch
    mixer_io = list(zip(mixer_specs, mixer_inputs)) + list(zip(mixer_out_specs, mixer_out_shapes))
    block_bytes = ((len(xs) + len(row_shapes)) * _nbytes((tile, d), F32)
                   + sum(_spec_bytes(sp, a.dtype) for sp, a in mixer_io if sp.pipeline_mode is None))
    resident_bytes = (_nbytes(w1.shape[1:], w1.dtype) + _nbytes(w2.shape[1:], w2.dtype)
                      + sum(_spec_bytes(sp, a.dtype) for sp, a in mixer_io if sp.pipeline_mode is not None))
    scratch_bytes = sum(_nbytes(s, t) for s, t in scratch)
    temp_bytes = _nbytes((tile, FF_CHUNK), F32) + 2 * _nbytes((tile, d), F32)

    return pl.pallas_call(
        functools.partial(body, geo, len(xs), final),
        grid=(geo.steps,),
        in_specs=x_specs + mixer_specs + ffn_specs,
        out_specs=row_specs + mixer_out_specs,
        out_shape=row_shapes + mixer_out_shapes,
        scratch_shapes=[pltpu.VMEM(s, t) for s, t in scratch],
        compiler_params=pltpu.CompilerParams(
            dimension_semantics=("arbitrary",),
            vmem_limit_bytes=_vmem_limit(block_bytes, resident_bytes, scratch_bytes, temp_bytes)),
        name=f"{name[0]}_layer_{name[1]}",
    )(*xs, *mixer_inputs, g_ffn, w1, w2, g_final)


def _segment_maps(geo):
    pt = geo.prompt_tiles

    def sample_tile(s):
        return jnp.maximum(geo.mix_tile(s) - pt, 0)

    def prompt_seq(s):
        i = geo.mix_tile(s)
        return jnp.where(i < pt, i // geo.tiles_per_seq, geo.n_prompt_seq)

    return sample_tile, prompt_seq


def _even_layer(geo, layer, xs, final, g_mix, w_in, w_out, wmix, bmix, ln_g, ln_b, pool_w, b_scale,
                hist_pool, ffn_inputs):
    li = layer // 2
    tile, spt, seg, pt = geo.tile, geo.seqs_per_tile, geo.sample_len, geo.prompt_tiles
    in_w = w_in.shape[2]
    aw = w_out.shape[1] // 2
    bw = in_w - 2 * aw
    n_sample_seq = hist_pool.shape[1]
    sample_tile, prompt_seq = _segment_maps(geo)

    mixer_specs = [
        _resident((1, geo.d_model)),
        _resident((None,) + w_in.shape[1:], (li, 0, 0)),
        _resident((None,) + w_out.shape[1:], (li, 0, 0)),
        pl.BlockSpec((1,) + wmix.shape[1:], lambda s: (geo.mix_tile(s) // pt, 0, 0, 0)),
        pl.BlockSpec((1,) + bmix.shape[1:], lambda s: (geo.mix_tile(s) // pt, 0, 0)),
        _resident((1, aw)),
        _resident((1, aw)),
        _resident((None,) + pool_w.shape[1:], (li, 0, 0, 0)),
        _resident((1, bw)),
        pl.BlockSpec((None, spt, POOL_PAD, bw), lambda s: (li, sample_tile(s), 0, 0)),
    ]
    out_shapes = [
        jax.ShapeDtypeStruct((geo.sample_tiles * tile, aw), F32),
        jax.ShapeDtypeStruct((geo.n_prompt_seq + 1, POOL_HIST, bw), F32),
        jax.ShapeDtypeStruct((n_sample_seq, POOL_HIST, bw), F32),
    ]
    out_specs = [
        pl.BlockSpec((tile, aw), lambda s: (sample_tile(s), 0)),
        pl.BlockSpec((1, POOL_HIST, bw), lambda s: (prompt_seq(s), 0, 0)),
        pl.BlockSpec((spt, POOL_HIST, bw), lambda s: (sample_tile(s), 0, 0)),
    ]
    scratch = [
        ((spt, POOL_PAD + seg, bw), F32),
        ((POOL_PAD, bw), F32),
        ((tile, bw), F32),
        ((tile, aw + bw), BF16),
    ]
    outs = _layer_call(
        geo, ("even", layer), _even_layer_kernel, xs, final,
        (g_mix, w_in, w_out, wmix, bmix, ln_g, ln_b, pool_w, b_scale, hist_pool), mixer_specs, ffn_inputs,
        out_shapes, out_specs, scratch, in_w)
    n_rows = 2 if final else 1
    v_s, pool_p, pool_s = outs[n_rows:]
    return outs[:n_rows], v_s, pool_p[:geo.n_prompt_seq], pool_s


def _odd_layer(geo, layer, xs, final, g_mix, w_in, w_out, conv_w, conv_b, ln_g, ln_b, dconv_w,
               hist_c, hist_d, ffn_inputs):
    li = layer // 2
    tile, spt, seg = geo.tile, geo.seqs_per_tile, geo.sample_len
    in_w = w_in.shape[2]
    cwid = w_out.shape[1] // 2
    kc, kd = conv_w.shape[1], dconv_w.shape[1]
    hc, hd = kc - 1, kd - 1
    pc, pd = hist_c.shape[2], hist_d.shape[2]
    n_sample_seq = hist_c.shape[1]
    sample_tile, prompt_seq = _segment_maps(geo)

    mixer_specs = [
        _resident((1, geo.d_model)),
        _resident((None,) + w_in.shape[1:], (li, 0, 0)),
        _resident((None,) + w_out.shape[1:], (li, 0, 0)),
        _resident((None,) + conv_w.shape[1:], (li, 0, 0)),
        _resident((1, cwid)),
        _resident((1, cwid)),
        _resident((1, cwid)),
        _resident((None,) + dconv_w.shape[1:], (li, 0, 0)),
        pl.BlockSpec((None, spt, pc, cwid), lambda s: (li, sample_tile(s), 0, 0)),
        pl.BlockSpec((None, spt, pd, cwid), lambda s: (li, sample_tile(s), 0, 0)),
    ]
    out_shapes = [
        jax.ShapeDtypeStruct((geo.n_prompt_seq + 1, hc, cwid), F32),
        jax.ShapeDtypeStruct((geo.n_prompt_seq + 1, hd, cwid), F32),
        jax.ShapeDtypeStruct((n_sample_seq, hc, cwid), F32),
        jax.ShapeDtypeStruct((n_sample_seq, hd, cwid), F32),
    ]
    out_specs = [
        pl.BlockSpec((1, hc, cwid), lambda s: (prompt_seq(s), 0, 0)),
        pl.BlockSpec((1, hd, cwid), lambda s: (prompt_seq(s), 0, 0)),
        pl.BlockSpec((spt, hc, cwid), lambda s: (sample_tile(s), 0, 0)),
        pl.BlockSpec((spt, hd, cwid), lambda s: (sample_tile(s), 0, 0)),
    ]
    scratch = [
        ((spt, pc + seg, cwid), F32),
        ((spt, pd + seg, cwid), F32),
        ((pc, cwid), F32),
        ((pd, cwid), F32),
        ((tile, cwid), F32),
        ((tile, 2 * cwid), BF16),
    ]
    outs = _layer_call(
        geo, ("odd", layer), _odd_layer_kernel, xs, final,
        (g_mix, w_in, w_out, conv_w, conv_b, ln_g, ln_b, dconv_w, hist_c, hist_d), mixer_specs, ffn_inputs,
        out_shapes, out_specs, scratch, in_w)
    n_rows = 2 if final else 1
    c_p, d_p, c_s, d_s = outs[n_rows:]
    return outs[:n_rows], c_p[:geo.n_prompt_seq], d_p[:geo.n_prompt_seq], c_s, d_s


def _gating_operands(w_s, b_s, sample_len):
    heads, blk, _ = w_s.shape
    cidx = jnp.arange(blk) // CHUNK
    mask = (cidx[None, :] <= cidx[:, None]).astype(w_s.dtype)
    w_prompt = w_s * mask[None]
    head_dim_lanes = LANES // 2
    reps = blk // sample_len
    eye = jnp.eye(reps, dtype=w_s.dtype)
    w_first = w_prompt[:, :sample_len, :sample_len]
    w_sample = jnp.einsum("ab,hij->haibj", eye, w_first).reshape(heads, blk, blk)
    b_sample = jnp.tile(b_s[:, :sample_len], (1, reps))

    def pairs(w):
        return jnp.concatenate([w[0::2], w[1::2]], axis=2)

    def bias(b):
        return jnp.repeat(b.T, head_dim_lanes, axis=1)

    wmix = jnp.stack([pairs(w_prompt), pairs(w_sample)]).astype(BF16)
    bmix = jnp.stack([bias(b_s), bias(b_sample)]).astype(F32)
    return wmix, bmix


def _pad_history(state):
    rows = state.shape[2]
    return jnp.pad(state, ((0, 0), (0, 0), (_round_up(rows, SUBLANES) - rows, 0), (0, 0)))


def kernel(x_prompt, x_sample, state_pool, state_conv_c, state_conv_d, norm_mix_g, norm_ffn_g, final_norm_g,
           w_in_even, w_out_even, a_w_s, a_b_s, a_ln_g, a_ln_b, b_pool_w, b_scale,
           w_in_odd, w_out_odd, c_conv_w, c_conv_b, c_ln_g, c_ln_b, d_conv_w, w_ff1, w_ff2):
    n_prompt_seq, seq, d = x_prompt.shape
    n_sample_seq, sample_len, _ = x_sample.shape
    depth = norm_mix_g.shape[0]
    tile = ROW_TILE
    assert seq % tile == 0 and tile % A_BLOCK == 0 and tile % sample_len == 0
    assert A_BLOCK % sample_len == 0 and sample_len <= CHUNK
    assert (n_sample_seq * sample_len) % tile == 0
    assert a_w_s.shape[1] == A_HEADS and a_w_s.shape[2] == A_BLOCK
    assert w_out_even.shape[1] // 2 == A_HEADS * (LANES // 2)
    assert state_pool.shape[2] == POOL_HIST and sample_len >= _round_up(c_conv_w.shape[1] - 1, SUBLANES)
    geo = Geometry(
        d_model=d, tile=tile,
        prompt_tiles=n_prompt_seq * seq // tile, tiles_per_seq=seq // tile, n_prompt_seq=n_prompt_seq,
        sample_tiles=n_sample_seq * sample_len // tile, sample_len=sample_len,
        seqs_per_tile=tile // sample_len)

    def row(v):
        return v.reshape(1, -1)

    w_in_even_b, w_out_even_b = w_in_even.astype(BF16), w_out_even.astype(BF16)
    w_in_odd_b, w_out_odd_b = w_in_odd.astype(BF16), w_out_odd.astype(BF16)
    pool_w_b = b_pool_w.astype(BF16)
    w_ff1_b, w_ff2_b = w_ff1.astype(BF16), w_ff2.astype(BF16)
    hist_pool, hist_c, hist_d = _pad_history(state_pool), _pad_history(state_conv_c), _pad_history(state_conv_d)

    xs = (x_prompt.reshape(-1, d), x_sample.reshape(-1, d))
    pools_p, pools_s, vs, cs_p, cs_s, ds_p, ds_s = [], [], [], [], [], [], []
    for layer in range(depth):
        li = layer // 2
        final = layer == depth - 1
        ffn_inputs = (row(norm_ffn_g[layer]), w_ff1_b, w_ff2_b, row(final_norm_g))
        if layer % 2 == 0:
            wmix, bmix = _gating_operands(a_w_s[li], a_b_s[li], sample_len)
            xs, v_s, pool_p, pool_s = _even_layer(
                geo, layer, xs, final, row(norm_mix_g[layer]), w_in_even_b, w_out_even_b, wmix, bmix,
                row(a_ln_g[li]), row(a_ln_b[li]), pool_w_b, row(b_scale[li]), hist_pool, ffn_inputs)
            vs.append(v_s.reshape(n_sample_seq, sample_len, -1))
            pools_p.append(pool_p)
            pools_s.append(pool_s)
        else:
            xs, c_p, d_p, c_s, d_s = _odd_layer(
                geo, layer, xs, final, row(norm_mix_g[layer]), w_in_odd_b, w_out_odd_b, c_conv_w,
                row(c_conv_b[li]), row(c_ln_g[li]), row(c_ln_b[li]), d_conv_w, hist_c, hist_d, ffn_inputs)
            cs_p.append(c_p)
            ds_p.append(d_p)
            cs_s.append(c_s)
            ds_s.append(d_s)

    y_prompt = xs[0].reshape(n_prompt_seq, seq, d)
    y_sample = xs[1].reshape(n_sample_seq, sample_len, d)
    return (y_prompt, y_sample, jnp.stack(pools_p), jnp.stack(cs_p), jnp.stack(ds_p),
            jnp.stack(vs), jnp.stack(pools_s), jnp.stack(cs_s), jnp.stack(ds_s))
```

```python
import functools
from typing import NamedTuple

import jax
import jax.numpy as jnp
from jax import lax
from jax.experimental import pallas as pl
from jax.experimental.pallas import tpu as pltpu

F32 = jnp.float32
BF16 = jnp.bfloat16

EPS = 1e-6
PAST_LEN = 4096
CHUNK = 64
A_BLOCK = 128
A_HEADS = 8
POOL_WINDOWS = (2, 4, 8, 16)
LANES = 128
SUBLANES = 8
VMEM_CAPACITY_BYTES = 64 * 1024 * 1024
ROW_TILE = 512
FF_CHUNK = 1024
CONV_ROWS = 64
CAST_STEPS = 32

POOL_HIST = max(POOL_WINDOWS) - 1


def _round_up(n, m):
    return -(-n // m) * m


POOL_PAD = _round_up(POOL_HIST, SUBLANES)


class Geometry(NamedTuple):
    d_model: int
    tile: int
    prompt_tiles: int
    tiles_per_seq: int
    n_prompt_seq: int
    sample_tiles: int
    sample_len: int
    seqs_per_tile: int

    @property
    def tiles(self):
        return self.prompt_tiles + self.sample_tiles

    @property
    def steps(self):
        return self.tiles + 1

    def mix_tile(self, s):
        return jnp.minimum(s, self.tiles - 1)

    def ffn_tile(self, s):
        return jnp.maximum(s - 1, 0)


def _resident(shape, index=None):
    full = (0,) * len(shape) if index is None else tuple(index)
    return pl.BlockSpec(shape, lambda s: full, pipeline_mode=pl.Buffered(1))


def _spec_bytes(spec, dtype):
    return _nbytes([b for b in spec.block_shape if b is not None], dtype)


def _nbytes(shape, dtype):
    n = 1
    for s in shape:
        n *= s
    return n * jnp.dtype(dtype).itemsize


def _vmem_limit(block_bytes, resident_bytes, scratch_bytes, temp_bytes):
    need = 2 * block_bytes + resident_bytes + scratch_bytes + temp_bytes
    assert need <= VMEM_CAPACITY_BYTES, need
    return min(VMEM_CAPACITY_BYTES - (2 << 20), _round_up(need + need // 8, 1 << 20))


def _rms_norm(x, g):
    y = x * lax.rsqrt(jnp.mean(x * x, axis=-1, keepdims=True) + EPS)
    return y * g


def _layer_norm(x, g, b):
    mu = jnp.mean(x, axis=-1, keepdims=True)
    xc = x - mu
    y = xc * lax.rsqrt(jnp.mean(xc * xc, axis=-1, keepdims=True) + EPS)
    return y * g + b


def _gelu_tanh(x):
    c = 0.7978845608028654
    return x * (0.5 * (1.0 + jnp.tanh(c * (x + 0.044715 * (x * x * x)))))


def _load_rows(x_refs, is_prompt):
    if len(x_refs) == 1:
        return x_refs[0][...]
    return jnp.where(is_prompt, x_refs[0][...], x_refs[1][...])


def _tile_ids(geo):
    step = pl.program_id(0)
    m = geo.mix_tile(step)
    return step, m < geo.prompt_tiles, m % geo.tiles_per_seq


def _derived_zero(x):
    u = pltpu.bitcast(x, jnp.uint32)
    u = lax.shift_right_logical(lax.shift_right_logical(u, jnp.uint32(16)), jnp.uint32(16))
    return u.astype(F32)


def _order_after(ref, token, lane_tiles=(0,)):
    rows = SUBLANES * (4 // jnp.dtype(ref.dtype).itemsize)
    z = _derived_zero(token)
    z = jnp.concatenate([z] * (rows // SUBLANES), axis=0).astype(ref.dtype)
    for lt in lane_tiles:
        idx = (slice(0, rows), slice(lt * LANES, (lt + 1) * LANES))
        ref[idx] = ref[idx] + z


def _ffn_stages(x1_ref, g_ref, w1_ref, w2_ref, acc_ref, h_ref, t_ref, dst_ref=None):
    n_chunks = w1_ref.shape[1] // FF_CHUNK

    def up(c):
        def run(after=None):
            if c == 0:
                x = x1_ref[...]
                h_ref[...] = _rms_norm(x, g_ref[...]).astype(BF16)
                acc_ref[...] = x
            if after is not None:
                _order_after(h_ref, after)
            cols = slice(c * FF_CHUNK, (c + 1) * FF_CHUNK)
            t = jnp.maximum(jnp.dot(h_ref[...], w1_ref[:, cols], preferred_element_type=F32), 0.0)
            t_ref[...] = (t * t).astype(BF16)
            return t[-SUBLANES:, -LANES:]
        return run

    def down(c):
        def run(after=None):
            if after is not None:
                _order_after(t_ref, after)
            cols = slice(c * FF_CHUNK, (c + 1) * FF_CHUNK)
            d = jnp.dot(t_ref[...], w2_ref[cols, :], preferred_element_type=F32)
            if dst_ref is not None and c == n_chunks - 1:
                dst_ref[...] = acc_ref[...] + d
            else:
                acc_ref[...] += d
            return d[-SUBLANES:, -LANES:]
        return run

    return [stage(c) for c in range(n_chunks) for stage in (up, down)]


def _run_interleaved(*stage_lists):
    order = sorted(((k + 0.5) / len(stages), n, k) for n, stages in enumerate(stage_lists) for k in range(len(stages)))
    for _, n, k in order:
        stage_lists[n][k]()


def _project(x_refs, is_prompt, g_ref, win_ref, xres_ref, z_ref):
    x = _load_rows(x_refs, is_prompt)
    xres_ref[...] = x
    z_ref[...] = jnp.dot(_rms_norm(x, g_ref[...]).astype(BF16), win_ref[...], preferred_element_type=F32)


def _emit_rows(geo, step, final, acc_ref, gfin_ref, out_refs):
    if not final:
        return
    is_prompt = geo.ffn_tile(step) < geo.prompt_tiles

    @pl.when(is_prompt)
    def _():
        out_refs[0][...] = _rms_norm(acc_ref[...], gfin_ref[...])

    @pl.when(jnp.logical_not(is_prompt))
    def _():
        out_refs[1][...] = _rms_norm(acc_ref[...], gfin_ref[...])


def _pool_segment(ext, length, pos0, out_ref, out_base):
    pos = pos0 + lax.broadcasted_iota(jnp.int32, (length, 1), 0)
    for g, w in enumerate(POOL_WINDOWS):
        lanes = slice(g * LANES, (g + 1) * LANES)
        win = ext[0:POOL_PAD + length, lanes]
        s = win
        span = 1
        while span < w:
            s = s + pltpu.roll(s, span, axis=0)
            span *= 2
        inv_cnt = 1.0 / jnp.minimum(w, pos + 1).astype(F32)
        out_ref[out_base:out_base + length, lanes] = s[POOL_PAD:] * inv_cnt - win[POOL_PAD:]


def _cast_next_weights(refs, n_in, n_outs, n_cast):
    src = refs[n_in:n_in + n_cast]
    dst = refs[n_in + n_cast + n_outs:n_in + n_cast + n_outs + n_cast]
    for i_ref, o_ref in zip(src, dst):
        o_ref[...] = i_ref[...].astype(BF16)
    return refs[:n_in] + refs[n_in + n_cast:n_in + n_cast + n_outs] + refs[n_in + 2 * n_cast + n_outs:]


def _even_layer_kernel(geo, n_x, final, n_cast, *refs):
    n_out = 2 if final else 1
    refs = _cast_next_weights(refs, n_x + 14, n_out + 3, n_cast)
    x_refs = refs[:n_x]
    (g_ref, win_ref, wout_ref, wmix_ref, bmix_ref, lng_ref, lnb_ref, pw_ref, bscale_ref, hist_ref,
     gf_ref, w1_ref, w2_ref, gfin_ref) = refs[n_x:n_x + 14]
    out_refs = refs[n_x + 14:n_x + 14 + n_out]
    (v_ref, poolp_ref, pools_ref,
     x1_ref, acc_ref, z_ref, xres_ref, h_ref, t_ref, ext_ref, carry_ref, pooled_ref,
     ycat_ref) = refs[n_x + 14 + n_out:]
    step, is_prompt, j = _tile_ids(geo)
    tile, seg = geo.tile, geo.sample_len
    aw = wout_ref.shape[0] // 2

    @pl.when(step == 0)
    def _():
        x1_ref[...] = jnp.zeros(x1_ref.shape, F32)
        carry_ref[...] = jnp.zeros(carry_ref.shape, F32)

    ffn = _ffn_stages(x1_ref, gf_ref, w1_ref, w2_ref, acc_ref, h_ref, t_ref, None if final else out_refs[0])
    _project(x_refs, is_prompt, g_ref, win_ref, xres_ref, z_ref)

    fresh = j == 0

    def pool_stage(s):
        def run():
            rows = slice(s * seg, (s + 1) * seg)
            if s == 0:
                before = jnp.where(fresh, 0.0, carry_ref[...])
            else:
                before = z_ref[s * seg - POOL_PAD:s * seg, 2 * aw:]
            ext_ref[s, 0:POOL_PAD, :] = jnp.where(is_prompt, before, hist_ref[s])
            ext_ref[s, POOL_PAD:, :] = z_ref[rows, 2 * aw:]
            pos0 = jnp.where(is_prompt, j * tile + s * seg, PAST_LEN)
            _pool_segment(ext_ref.at[s], seg, pos0, pooled_ref, s * seg)
        return run

    lane = lax.broadcasted_iota(jnp.int32, (A_BLOCK, LANES), 1)
    low_head = lane < (LANES // 2)
    n_pairs = aw // LANES

    def gate_stage(b):
        def run():
            blocks = (slice(b * A_BLOCK, (b + 1) * A_BLOCK), slice((b + 1) * A_BLOCK, (b + 2) * A_BLOCK))
            vs = []
            for rows in blocks:
                v = _layer_norm(_gelu_tanh(z_ref[rows, aw:2 * aw]), lng_ref[...], lnb_ref[...])
                v_ref[rows, :] = v
                vs.append(v)
            mixed = ([], [])
            for p in range(n_pairs):
                rhs = []
                for v in vs:
                    vp = v[:, p * LANES:(p + 1) * LANES].astype(BF16)
                    zero = jnp.zeros_like(vp)
                    rhs.append(jnp.concatenate([jnp.where(low_head, vp, zero), jnp.where(low_head, zero, vp)],
                                               axis=0))
                both = jnp.dot(wmix_ref[0, p], jnp.concatenate(rhs, axis=1), preferred_element_type=F32)
                mixed[0].append(both[:, :LANES])
                mixed[1].append(both[:, LANES:])
            for rows, parts in zip(blocks, mixed):
                u = _gelu_tanh(z_ref[rows, 0:aw])
                ycat_ref[rows, 0:aw] = (u * (jnp.concatenate(parts, axis=1) + bmix_ref[0])).astype(BF16)
        return run

    mixer = ([pool_stage(s) for s in range(geo.seqs_per_tile)]
             + [gate_stage(b) for b in range(0, tile // A_BLOCK, 2)])
    _run_interleaved(mixer, ffn)
    carry_ref[...] = z_ref[tile - POOL_PAD:tile, 2 * aw:]

    for g in range(len(POOL_WINDOWS)):
        lanes = slice(g * LANES, (g + 1) * LANES)
        yb = jnp.dot(pooled_ref[:, lanes].astype(BF16), pw_ref[g], preferred_element_type=F32)
        ycat_ref[:, aw + g * LANES:aw + (g + 1) * LANES] = (yb * bscale_ref[:, lanes]).astype(BF16)

    y = jnp.dot(ycat_ref[...], wout_ref[...], preferred_element_type=F32)
    x1_ref[...] = xres_ref[...] + y

    tail = slice(POOL_PAD + seg - POOL_HIST, POOL_PAD + seg)
    for s in range(geo.seqs_per_tile):
        pools_ref[s] = ext_ref[s, tail, :]
    poolp_ref[0] = ext_ref[geo.seqs_per_tile - 1, tail, :]
    _emit_rows(geo, step, final, acc_ref, gfin_ref, out_refs)


def _dwconv_rows(ext, first_row, w_ref, n_taps, r0, length, lanes):
    wlen = length + _round_up(first_row + n_taps - 1, SUBLANES)
    win = ext[r0:r0 + wlen, lanes]
    acc = None
    for r in range(SUBLANES):
        taps = [k for k in range(n_taps) if (first_row + k) % SUBLANES == r]
        if not taps:
            continue
        shifted = win if r == 0 else pltpu.roll(win, wlen - r, axis=0)
        for k in taps:
            q0 = (first_row + k) // SUBLANES * SUBLANES
            term = shifted[q0:q0 + length] * w_ref[k:k + 1, lanes]
            acc = term if acc is None else acc + term
    return acc


def _odd_layer_kernel(geo, n_x, final, n_cast, *refs):
    n_out = 2 if final else 1
    refs = _cast_next_weights(refs, n_x + 14, n_out + 4, n_cast)
    x_refs = refs[:n_x]
    (g_ref, win_ref, wout_ref, cw_ref, cb_ref, lng_ref, lnb_ref, dw_ref, histc_ref, histd_ref,
     gf_ref, w1_ref, w2_ref, gfin_ref) = refs[n_x:n_x + 14]
    out_refs = refs[n_x + 14:n_x + 14 + n_out]
    (cp_ref, dp_ref, cs_ref, ds_ref,
     x1_ref, acc_ref, z_ref, xres_ref, h_ref, t_ref, extc_ref, extd_ref, carryc_ref, carryd_ref, conv_ref,
     ycat_ref) = refs[n_x + 14 + n_out:]
    step, is_prompt, j = _tile_ids(geo)
    tile, seg = geo.tile, geo.sample_len
    cwid = wout_ref.shape[0] // 2
    kc, kd = cw_ref.shape[0], dw_ref.shape[0]
    hc, hd = kc - 1, kd - 1
    pc, pd = carryc_ref.shape[0], carryd_ref.shape[0]
    o = 2 * cwid

    @pl.when(step == 0)
    def _():
        x1_ref[...] = jnp.zeros(x1_ref.shape, F32)
        carryc_ref[...] = jnp.zeros(carryc_ref.shape, F32)
        carryd_ref[...] = jnp.zeros(carryd_ref.shape, F32)

    ffn = _ffn_stages(x1_ref, gf_ref, w1_ref, w2_ref, acc_ref, h_ref, t_ref, None if final else out_refs[0])
    _project(x_refs, is_prompt, g_ref, win_ref, xres_ref, z_ref)

    fresh = j == 0

    def conv_stage(s):
        def run(after=None):
            rows = slice(s * seg, (s + 1) * seg)
            c_in = z_ref[rows, 0:cwid] * jax.nn.sigmoid(z_ref[rows, cwid:2 * cwid])
            d_in = z_ref[rows, o + cwid:o + 2 * cwid] * z_ref[rows, o + 2 * cwid:o + 3 * cwid]
            if s == 0:
                prev_c = jnp.where(fresh, 0.0, carryc_ref[...])
                prev_d = jnp.where(fresh, 0.0, carryd_ref[...])
            else:
                prev_c = extc_ref[s - 1, seg:, :]
                prev_d = extd_ref[s - 1, seg:, :]
            extc_ref[s, 0:pc, :] = jnp.where(is_prompt, prev_c, histc_ref[s])
            extd_ref[s, 0:pd, :] = jnp.where(is_prompt, prev_d, histd_ref[s])
            extc_ref[s, pc:, :] = c_in
            extd_ref[s, pd:, :] = d_in
            extc, extd = extc_ref.at[s], extd_ref.at[s]
            if after is not None:
                _order_after(extc, after, range(cwid // LANES))
                _order_after(extd, after, range(cwid // LANES))
            cacc = None
            for lt in range(cwid // LANES):
                lanes = slice(lt * LANES, (lt + 1) * LANES)
                for r0 in range(0, seg, CONV_ROWS):
                    out_rows = slice(s * seg + r0, s * seg + r0 + CONV_ROWS)
                    cacc = _dwconv_rows(extc, pc - hc, cw_ref, kc, r0, CONV_ROWS, lanes)
                    conv_ref[out_rows, lanes] = cacc + cb_ref[:, lanes]
                    d = _dwconv_rows(extd, pd - hd, dw_ref, kd, r0, CONV_ROWS, lanes)
                    ycat_ref[out_rows, cwid + lt * LANES:cwid + (lt + 1) * LANES] = (
                        z_ref[out_rows, o + lt * LANES:o + (lt + 1) * LANES] * d).astype(BF16)
            return cacc[:SUBLANES]
        return run

    def norm_stage(b):
        def run():
            rows = slice(b * A_BLOCK, (b + 1) * A_BLOCK)
            c = _layer_norm(conv_ref[rows, :], lng_ref[...], lnb_ref[...])
            ycat_ref[rows, 0:cwid] = (c * jax.nn.sigmoid(c)).astype(BF16)
        return run

    assert len(ffn) == geo.seqs_per_tile
    tok_f, tok_c = None, None
    for k in range(len(ffn)):
        new_f = ffn[k](after=tok_c)
        tok_c = conv_stage(k)(after=tok_f)
        tok_f = new_f
    for b in range(tile // A_BLOCK):
        norm_stage(b)()
    last = geo.seqs_per_tile - 1
    carryc_ref[...] = extc_ref[last, seg:, :]
    carryd_ref[...] = extd_ref[last, seg:, :]

    y = jnp.dot(ycat_ref[...], wout_ref[...], preferred_element_type=F32)
    x1_ref[...] = xres_ref[...] + y

    tail_c, tail_d = slice(pc + seg - hc, pc + seg), slice(pd + seg - hd, pd + seg)
    for s in range(geo.seqs_per_tile):
        cs_ref[s] = extc_ref[s, tail_c, :]
        ds_ref[s] = extd_ref[s, tail_d, :]
    cp_ref[0] = extc_ref[last, tail_c, :]
    dp_ref[0] = extd_ref[last, tail_d, :]
    _emit_rows(geo, step, final, acc_ref, gfin_ref, out_refs)


def _layer_call(geo, name, body, xs, final, mixer_inputs, mixer_specs, ffn_inputs,
                mixer_out_shapes, mixer_out_specs, mixer_scratch, in_w, next_weights):
    d, tile, pt = geo.d_model, geo.tile, geo.prompt_tiles
    g_ffn, w1, w2, g_final = ffn_inputs
    assert geo.steps >= CAST_STEPS
    cast_specs, cast_shapes = [], []
    for w in next_weights:
        rows = w.shape[0] // CAST_STEPS
        assert w.shape[0] % CAST_STEPS == 0 and rows % (2 * SUBLANES) == 0, w.shape
        cast_specs.append(pl.BlockSpec((rows, w.shape[1]), lambda s: (jnp.minimum(s, CAST_STEPS - 1), 0)))
        cast_shapes.append(jax.ShapeDtypeStruct(w.shape, BF16))
    ffn_tile = geo.ffn_tile
    if len(xs) == 1:
        x_specs = [pl.BlockSpec((tile, d), lambda s: (geo.mix_tile(s), 0))]
    else:
        x_specs = [pl.BlockSpec((tile, d), lambda s: (jnp.minimum(s, pt - 1), 0)),
                   pl.BlockSpec((tile, d), lambda s: (jnp.clip(s - pt, 0, geo.sample_tiles - 1), 0))]
    ffn_specs = [_resident((1, d)), _resident(w1.shape), _resident(w2.shape), _resident((1, d))]
    if final:
        row_shapes = [jax.ShapeDtypeStruct((pt * tile, d), F32),
                      jax.ShapeDtypeStruct((geo.sample_tiles * tile, d), F32)]
        row_specs = [pl.BlockSpec((tile, d), lambda s: (jnp.minimum(ffn_tile(s), pt - 1), 0)),
                     pl.BlockSpec((tile, d), lambda s: (jnp.maximum(ffn_tile(s) - pt, 0), 0))]
    else:
        row_shapes = [jax.ShapeDtypeStruct((geo.tiles * tile, d), F32)]
        row_specs = [pl.BlockSpec((tile, d), lambda s: (ffn_tile(s), 0))]
    scratch = [((tile, d), F32),
               ((tile, d), F32),
               ((tile, in_w), F32),
               ((tile, d), F32),
               ((tile, d), BF16),
               ((tile, FF_CHUNK), BF16)] + mixer_scratch
    mixer_io = list(zip(mixer_specs, mixer_inputs)) + list(zip(mixer_out_specs, mixer_out_shapes))
    block_bytes = ((len(xs) + len(row_shapes)) * _nbytes((tile, d), F32)
                   + sum(_spec_bytes(sp, a.dtype) for sp, a in mixer_io if sp.pipeline_mode is None)
                   + sum(_spec_bytes(sp, F32) + _spec_bytes(sp, BF16) for sp in cast_specs))
    resident_bytes = (_nbytes(w1.shape, w1.dtype) + _nbytes(w2.shape, w2.dtype)
                      + sum(_spec_bytes(sp, a.dtype) for sp, a in mixer_io if sp.pipeline_mode is not None))
    scratch_bytes = sum(_nbytes(s, t) for s, t in scratch)
    temp_bytes = _nbytes((tile, FF_CHUNK), F32) + 2 * _nbytes((tile, d), F32)

    return pl.pallas_call(
        functools.partial(body, geo, len(xs), final, len(next_weights)),
        grid=(geo.steps,),
        in_specs=x_specs + mixer_specs + ffn_specs + cast_specs,
        out_specs=row_specs + mixer_out_specs + cast_specs,
        out_shape=row_shapes + mixer_out_shapes + cast_shapes,
        scratch_shapes=[pltpu.VMEM(s, t) for s, t in scratch],
        compiler_params=pltpu.CompilerParams(
            dimension_semantics=("arbitrary",),
            vmem_limit_bytes=_vmem_limit(block_bytes, resident_bytes, scratch_bytes, temp_bytes)),
        name=f"{name[0]}_layer_{name[1]}",
    )(*xs, *mixer_inputs, g_ffn, w1, w2, g_final, *next_weights)


def _segment_maps(geo):
    pt = geo.prompt_tiles

    def sample_tile(s):
        return jnp.maximum(geo.mix_tile(s) - pt, 0)

    def prompt_seq(s):
        i = geo.mix_tile(s)
        return jnp.where(i < pt, i // geo.tiles_per_seq, geo.n_prompt_seq)

    return sample_tile, prompt_seq


def _even_layer(geo, layer, xs, final, g_mix, w_in, w_out, wmix, bmix, ln_g, ln_b, pool_w, b_scale,
                hist_pool, ffn_inputs, next_weights):
    li = layer // 2
    tile, spt, seg, pt = geo.tile, geo.seqs_per_tile, geo.sample_len, geo.prompt_tiles
    in_w = w_in.shape[1]
    aw = w_out.shape[0] // 2
    bw = in_w - 2 * aw
    n_sample_seq = hist_pool.shape[1]
    sample_tile, prompt_seq = _segment_maps(geo)

    mixer_specs = [
        _resident((1, geo.d_model)),
        _resident(w_in.shape),
        _resident(w_out.shape),
        pl.BlockSpec((1,) + wmix.shape[1:], lambda s: (geo.mix_tile(s) // pt, 0, 0, 0)),
        pl.BlockSpec((1,) + bmix.shape[1:], lambda s: (geo.mix_tile(s) // pt, 0, 0)),
        _resident((1, aw)),
        _resident((1, aw)),
        _resident(pool_w.shape),
        _resident((1, bw)),
        pl.BlockSpec((None, spt, POOL_PAD, bw), lambda s: (li, sample_tile(s), 0, 0)),
    ]
    out_shapes = [
        jax.ShapeDtypeStruct((geo.sample_tiles * tile, aw), F32),
        jax.ShapeDtypeStruct((geo.n_prompt_seq + 1, POOL_HIST, bw), F32),
        jax.ShapeDtypeStruct((n_sample_seq, POOL_HIST, bw), F32),
    ]
    out_specs = [
        pl.BlockSpec((tile, aw), lambda s: (sample_tile(s), 0)),
        pl.BlockSpec((1, POOL_HIST, bw), lambda s: (prompt_seq(s), 0, 0)),
        pl.BlockSpec((spt, POOL_HIST, bw), lambda s: (sample_tile(s), 0, 0)),
    ]
    scratch = [
        ((spt, POOL_PAD + seg, bw), F32),
        ((POOL_PAD, bw), F32),
        ((tile, bw), F32),
        ((tile, aw + bw), BF16),
    ]
    outs = _layer_call(
        geo, ("even", layer), _even_layer_kernel, xs, final,
        (g_mix, w_in, w_out, wmix, bmix, ln_g, ln_b, pool_w, b_scale, hist_pool), mixer_specs, ffn_inputs,
        out_shapes, out_specs, scratch, in_w, next_weights)
    n_rows = 2 if final else 1
    v_s, pool_p, pool_s = outs[n_rows:n_rows + 3]
    return outs[:n_rows], v_s, pool_p[:geo.n_prompt_seq], pool_s, outs[n_rows + 3:]


def _odd_layer(geo, layer, xs, final, g_mix, w_in, w_out, conv_w, conv_b, ln_g, ln_b, dconv_w,
               hist_c, hist_d, ffn_inputs, next_weights):
    li = layer // 2
    tile, spt, seg = geo.tile, geo.seqs_per_tile, geo.sample_len
    in_w = w_in.shape[1]
    cwid = w_out.shape[0] // 2
    kc, kd = conv_w.shape[0], dconv_w.shape[0]
    hc, hd = kc - 1, kd - 1
    pc, pd = hist_c.shape[2], hist_d.shape[2]
    n_sample_seq = hist_c.shape[1]
    sample_tile, prompt_seq = _segment_maps(geo)

    mixer_specs = [
        _resident((1, geo.d_model)),
        _resident(w_in.shape),
        _resident(w_out.shape),
        _resident(conv_w.shape),
        _resident((1, cwid)),
        _resident((1, cwid)),
        _resident((1, cwid)),
        _resident(dconv_w.shape),
        pl.BlockSpec((None, spt, pc, cwid), lambda s: (li, sample_tile(s), 0, 0)),
        pl.BlockSpec((None, spt, pd, cwid), lambda s: (li, sample_tile(s), 0, 0)),
    ]
    out_shapes = [
        jax.ShapeDtypeStruct((geo.n_prompt_seq + 1, hc, cwid), F32),
        jax.ShapeDtypeStruct((geo.n_prompt_seq + 1, hd, cwid), F32),
        jax.ShapeDtypeStruct((n_sample_seq, hc, cwid), F32),
        jax.ShapeDtypeStruct((n_sample_seq, hd, cwid), F32),
    ]
    out_specs = [
        pl.BlockSpec((1, hc, cwid), lambda s: (prompt_seq(s), 0, 0)),
        pl.BlockSpec((1, hd, cwid), lambda s: (prompt_seq(s), 0, 0)),
        pl.BlockSpec((spt, hc, cwid), lambda s: (sample_tile(s), 0, 0)),
        pl.BlockSpec((spt, hd, cwid), lambda s: (sample_tile(s), 0, 0)),
    ]
    scratch = [
        ((spt, pc + seg, cwid), F32),
        ((spt, pd + seg, cwid), F32),
        ((pc, cwid), F32),
        ((pd, cwid), F32),
        ((tile, cwid), F32),
        ((tile, 2 * cwid), BF16),
    ]
    outs = _layer_call(
        geo, ("odd", layer), _odd_layer_kernel, xs, final,
        (g_mix, w_in, w_out, conv_w, conv_b, ln_g, ln_b, dconv_w, hist_c, hist_d), mixer_specs, ffn_inputs,
        out_shapes, out_specs, scratch, in_w, next_weights)
    n_rows = 2 if final else 1
    c_p, d_p, c_s, d_s = outs[n_rows:n_rows + 4]
    return outs[:n_rows], c_p[:geo.n_prompt_seq], d_p[:geo.n_prompt_seq], c_s, d_s, outs[n_rows + 4:]


def _gating_operands(w_s, b_s, sample_len):
    heads, blk, _ = w_s.shape
    cidx = jnp.arange(blk) // CHUNK
    mask = (cidx[None, :] <= cidx[:, None]).astype(w_s.dtype)
    w_prompt = w_s * mask[None]
    head_dim_lanes = LANES // 2
    reps = blk // sample_len
    eye = jnp.eye(reps, dtype=w_s.dtype)
    w_first = w_prompt[:, :sample_len, :sample_len]
    w_sample = jnp.einsum("ab,hij->haibj", eye, w_first).reshape(heads, blk, blk)
    b_sample = jnp.tile(b_s[:, :sample_len], (1, reps))

    def pairs(w):
        return jnp.concatenate([w[0::2], w[1::2]], axis=2)

    def bias(b):
        return jnp.repeat(b.T, head_dim_lanes, axis=1)

    wmix = jnp.stack([pairs(w_prompt), pairs(w_sample)]).astype(BF16)
    bmix = jnp.stack([bias(b_s), bias(b_sample)]).astype(F32)
    return wmix, bmix


def _pad_history(state):
    rows = state.shape[2]
    return jnp.pad(state, ((0, 0), (0, 0), (_round_up(rows, SUBLANES) - rows, 0), (0, 0)))


def kernel(x_prompt, x_sample, state_pool, state_conv_c, state_conv_d, norm_mix_g, norm_ffn_g, final_norm_g,
           w_in_even, w_out_even, a_w_s, a_b_s, a_ln_g, a_ln_b, b_pool_w, b_scale,
           w_in_odd, w_out_odd, c_conv_w, c_conv_b, c_ln_g, c_ln_b, d_conv_w, w_ff1, w_ff2):
    n_prompt_seq, seq, d = x_prompt.shape
    n_sample_seq, sample_len, _ = x_sample.shape
    depth = norm_mix_g.shape[0]
    tile = ROW_TILE
    assert seq % tile == 0 and tile % A_BLOCK == 0 and tile % sample_len == 0
    assert A_BLOCK % sample_len == 0 and sample_len <= CHUNK
    assert (n_sample_seq * sample_len) % tile == 0
    assert a_w_s.shape[1] == A_HEADS and a_w_s.shape[2] == A_BLOCK
    assert w_out_even.shape[1] // 2 == A_HEADS * (LANES // 2)
    assert state_pool.shape[2] == POOL_HIST and sample_len >= _round_up(c_conv_w.shape[1] - 1, SUBLANES)
    geo = Geometry(
        d_model=d, tile=tile,
        prompt_tiles=n_prompt_seq * seq // tile, tiles_per_seq=seq // tile, n_prompt_seq=n_prompt_seq,
        sample_tiles=n_sample_seq * sample_len // tile, sample_len=sample_len,
        seqs_per_tile=tile // sample_len)

    def row(v):
        return v.reshape(1, -1)

    pool_w_b = b_pool_w.astype(BF16)
    hist_pool, hist_c, hist_d = _pad_history(state_pool), _pad_history(state_conv_c), _pad_history(state_conv_d)

    def layer_weights(layer):
        li = layer // 2
        w_in, w_out = (w_in_even, w_out_even) if layer % 2 == 0 else (w_in_odd, w_out_odd)
        return w_in[li], w_out[li], w_ff1[layer], w_ff2[layer]

    weights_b = [w.astype(BF16) for w in layer_weights(0)]
    xs = (x_prompt.reshape(-1, d), x_sample.reshape(-1, d))
    pools_p, pools_s, vs, cs_p, cs_s, ds_p, ds_s = [], [], [], [], [], [], []
    for layer in range(depth):
        li = layer // 2
        final = layer == depth - 1
        w_in_b, w_out_b, w1_b, w2_b = weights_b
        next_weights = () if final else layer_weights(layer + 1)
        ffn_inputs = (row(norm_ffn_g[layer]), w1_b, w2_b, row(final_norm_g))
        if layer % 2 == 0:
            wmix, bmix = _gating_operands(a_w_s[li], a_b_s[li], sample_len)
            xs, v_s, pool_p, pool_s, weights_b = _even_layer(
                geo, layer, xs, final, row(norm_mix_g[layer]), w_in_b, w_out_b, wmix, bmix,
                row(a_ln_g[li]), row(a_ln_b[li]), pool_w_b[li], row(b_scale[li]), hist_pool, ffn_inputs,
                next_weights)
            vs.append(v_s.reshape(n_sample_seq, sample_len, -1))
            pools_p.append(pool_p)
            pools_s.append(pool_s)
        else:
            xs, c_p, d_p, c_s, d_s, weights_b = _odd_layer(
                geo, layer, xs, final, row(norm_mix_g[layer]), w_in_b, w_out_b, c_conv_w[li],
                row(c_conv_b[li]), row(c_ln_g[li]), row(c_ln_b[li]), d_conv_w[li], hist_c, hist_d, ffn_inputs,
                next_weights)
            cs_p.append(c_p)
            ds_p.append(d_p)
            cs_s.append(c_s)
            ds_s.append(d_s)

    y_prompt = xs[0].reshape(n_prompt_seq, seq, d)
    y_sample = xs[1].reshape(n_sample_seq, sample_len, d)
    return (y_prompt, y_sample, jnp.stack(pools_p), jnp.stack(cs_p), jnp.stack(ds_p),
            jnp.stack(vs), jnp.stack(pools_s), jnp.stack(cs_s), jnp.stack(ds_s))
```

```python
import functools
from typing import NamedTuple

import jax
import jax.numpy as jnp
from jax import lax
from jax.experimental import pallas as pl
from jax.experimental.pallas import tpu as pltpu

F32 = jnp.float32
BF16 = jnp.bfloat16

EPS = 1e-6
PAST_LEN = 4096
CHUNK = 64
A_BLOCK = 128
A_HEADS = 8
POOL_WINDOWS = (2, 4, 8, 16)
LANES = 128
SUBLANES = 8
VMEM_CAPACITY_BYTES = 64 * 1024 * 1024
ROW_TILE = 512
FF_CHUNK = 1024
CONV_ROWS = 64
CAST_STEPS = 32

POOL_HIST = max(POOL_WINDOWS) - 1


def _round_up(n, m):
    return -(-n // m) * m


POOL_PAD = _round_up(POOL_HIST, SUBLANES)


class Geometry(NamedTuple):
    d_model: int
    tile: int
    prompt_tiles: int
    tiles_per_seq: int
    n_prompt_seq: int
    sample_tiles: int
    sample_len: int
    seqs_per_tile: int

    @property
    def tiles(self):
        return self.prompt_tiles + self.sample_tiles

    @property
    def steps(self):
        return self.tiles + 1

    def mix_tile(self, s):
        return jnp.minimum(s, self.tiles - 1)

    def ffn_tile(self, s):
        return jnp.maximum(s - 1, 0)


def _resident(shape, index=None):
    full = (0,) * len(shape) if index is None else tuple(index)
    return pl.BlockSpec(shape, lambda s: full, pipeline_mode=pl.Buffered(1))


def _spec_bytes(spec, dtype):
    return _nbytes([b for b in spec.block_shape if b is not None], dtype)


def _nbytes(shape, dtype):
    n = 1
    for s in shape:
        n *= s
    return n * jnp.dtype(dtype).itemsize


def _vmem_limit(block_bytes, resident_bytes, scratch_bytes, temp_bytes):
    need = 2 * block_bytes + resident_bytes + scratch_bytes + temp_bytes
    assert need <= VMEM_CAPACITY_BYTES, need
    return min(VMEM_CAPACITY_BYTES - (2 << 20), _round_up(need + need // 8, 1 << 20))


def _rms_norm(x, g):
    y = x * lax.rsqrt(jnp.mean(x * x, axis=-1, keepdims=True) + EPS)
    return y * g


def _layer_norm(x, g, b):
    mu = jnp.mean(x, axis=-1, keepdims=True)
    xc = x - mu
    y = xc * lax.rsqrt(jnp.mean(xc * xc, axis=-1, keepdims=True) + EPS)
    return y * g + b


def _gelu_tanh(x):
    c = 0.7978845608028654
    return x * (0.5 * (1.0 + jnp.tanh(c * (x + 0.044715 * (x * x * x)))))


def _load_rows(x_refs, is_prompt):
    if len(x_refs) == 1:
        return x_refs[0][...]
    return jnp.where(is_prompt, x_refs[0][...], x_refs[1][...])


def _tile_ids(geo):
    step = pl.program_id(0)
    m = geo.mix_tile(step)
    return step, m < geo.prompt_tiles, m % geo.tiles_per_seq


def _derived_zero(x):
    u = pltpu.bitcast(x, jnp.uint32)
    u = lax.shift_right_logical(lax.shift_right_logical(u, jnp.uint32(16)), jnp.uint32(16))
    return u.astype(F32)


def _order_after(ref, token, lane_tiles=(0,)):
    rows = SUBLANES * (4 // jnp.dtype(ref.dtype).itemsize)
    z = _derived_zero(token)
    z = jnp.concatenate([z] * (rows // SUBLANES), axis=0).astype(ref.dtype)
    for lt in lane_tiles:
        idx = (slice(0, rows), slice(lt * LANES, (lt + 1) * LANES))
        ref[idx] = ref[idx] + z


def _ffn_stages(x1_ref, g_ref, w1_ref, w2_ref, acc_ref, h_ref, t_ref, dst_ref=None):
    n_chunks = w1_ref.shape[1] // FF_CHUNK

    def up(c):
        def run(after=None):
            if c == 0:
                x = x1_ref[...]
                h_ref[...] = _rms_norm(x, g_ref[...]).astype(BF16)
                acc_ref[...] = x
            if after is not None:
                _order_after(h_ref, after)
            cols = slice(c * FF_CHUNK, (c + 1) * FF_CHUNK)
            t = jnp.maximum(jnp.dot(h_ref[...], w1_ref[:, cols], preferred_element_type=F32), 0.0)
            t_ref[...] = (t * t).astype(BF16)
            return t[-SUBLANES:, -LANES:]
        return run

    def down(c):
        def run(after=None):
            if after is not None:
                _order_after(t_ref, after)
            cols = slice(c * FF_CHUNK, (c + 1) * FF_CHUNK)
            d = jnp.dot(t_ref[...], w2_ref[cols, :], preferred_element_type=F32)
            if dst_ref is not None and c == n_chunks - 1:
                dst_ref[...] = acc_ref[...] + d
            else:
                acc_ref[...] += d
            return d[-SUBLANES:, -LANES:]
        return run

    return [stage(c) for c in range(n_chunks) for stage in (up, down)]


def _run_interleaved(*stage_lists):
    order = sorted(((k + 0.5) / len(stages), n, k) for n, stages in enumerate(stage_lists) for k in range(len(stages)))
    for _, n, k in order:
        stage_lists[n][k]()


def _project(x_refs, is_prompt, g_ref, win_ref, xres_ref, z_ref):
    x = _load_rows(x_refs, is_prompt)
    xres_ref[...] = x
    z_ref[...] = jnp.dot(_rms_norm(x, g_ref[...]).astype(BF16), win_ref[...], preferred_element_type=F32)


def _emit_rows(geo, step, final, acc_ref, gfin_ref, out_refs):
    if not final:
        return
    is_prompt = geo.ffn_tile(step) < geo.prompt_tiles

    @pl.when(is_prompt)
    def _():
        out_refs[0][...] = _rms_norm(acc_ref[...], gfin_ref[...])

    @pl.when(jnp.logical_not(is_prompt))
    def _():
        out_refs[1][...] = _rms_norm(acc_ref[...], gfin_ref[...])


def _pool_segment(ext, length, pos0, out_ref, out_base):
    pos = pos0 + lax.broadcasted_iota(jnp.int32, (length, 1), 0)
    for g, w in enumerate(POOL_WINDOWS):
        lanes = slice(g * LANES, (g + 1) * LANES)
        win = ext[0:POOL_PAD + length, lanes]
        s = win
        span = 1
        while span < w:
            s = s + pltpu.roll(s, span, axis=0)
            span *= 2
        inv_cnt = 1.0 / jnp.minimum(w, pos + 1).astype(F32)
        out_ref[out_base:out_base + length, lanes] = s[POOL_PAD:] * inv_cnt - win[POOL_PAD:]


def _cast_next_weights(refs, n_in, n_outs, n_cast):
    src = refs[n_in:n_in + n_cast]
    dst = refs[n_in + n_cast + n_outs:n_in + n_cast + n_outs + n_cast]
    for i_ref, o_ref in zip(src, dst):
        o_ref[...] = i_ref[...].astype(BF16)
    return refs[:n_in] + refs[n_in + n_cast:n_in + n_cast + n_outs] + refs[n_in + 2 * n_cast + n_outs:]


def _even_layer_kernel(geo, n_x, final, n_cast, *refs):
    n_out = 2 if final else 1
    refs = _cast_next_weights(refs, n_x + 14, n_out + 3, n_cast)
    x_refs = refs[:n_x]
    (g_ref, win_ref, wout_ref, wmix_ref, bmix_ref, lng_ref, lnb_ref, pw_ref, bscale_ref, hist_ref,
     gf_ref, w1_ref, w2_ref, gfin_ref) = refs[n_x:n_x + 14]
    out_refs = refs[n_x + 14:n_x + 14 + n_out]
    (v_ref, poolp_ref, pools_ref,
     x1_ref, acc_ref, z_ref, xres_ref, h_ref, t_ref, ext_ref, carry_ref, pooled_ref,
     ycat_ref) = refs[n_x + 14 + n_out:]
    step, is_prompt, j = _tile_ids(geo)
    tile, seg = geo.tile, geo.sample_len
    aw = wout_ref.shape[0] // 2

    @pl.when(step == 0)
    def _():
        x1_ref[...] = jnp.zeros(x1_ref.shape, F32)
        carry_ref[...] = jnp.zeros(carry_ref.shape, F32)

    ffn = _ffn_stages(x1_ref, gf_ref, w1_ref, w2_ref, acc_ref, h_ref, t_ref, None if final else out_refs[0])
    _project(x_refs, is_prompt, g_ref, win_ref, xres_ref, z_ref)

    fresh = j == 0

    def pool_stage(s):
        def run():
            rows = slice(s * seg, (s + 1) * seg)
            if s == 0:
                before = jnp.where(fresh, 0.0, carry_ref[...])
            else:
                before = z_ref[s * seg - POOL_PAD:s * seg, 2 * aw:]
            ext_ref[s, 0:POOL_PAD, :] = jnp.where(is_prompt, before, hist_ref[s])
            ext_ref[s, POOL_PAD:, :] = z_ref[rows, 2 * aw:]
            pos0 = jnp.where(is_prompt, j * tile + s * seg, PAST_LEN)
            _pool_segment(ext_ref.at[s], seg, pos0, pooled_ref, s * seg)
        return run

    lane = lax.broadcasted_iota(jnp.int32, (A_BLOCK, LANES), 1)
    low_head = lane < (LANES // 2)
    n_pairs = aw // LANES

    def gate_stage(b):
        def run():
            blocks = (slice(b * A_BLOCK, (b + 1) * A_BLOCK), slice((b + 1) * A_BLOCK, (b + 2) * A_BLOCK))
            vs = []
            for rows in blocks:
                v = _layer_norm(_gelu_tanh(z_ref[rows, aw:2 * aw]), lng_ref[...], lnb_ref[...])
                v_ref[rows, :] = v
                vs.append(v)
            mixed = ([], [])
            for p in range(n_pairs):
                rhs = []
                for v in vs:
                    vp = v[:, p * LANES:(p + 1) * LANES].astype(BF16)
                    zero = jnp.zeros_like(vp)
                    rhs.append(jnp.concatenate([jnp.where(low_head, vp, zero), jnp.where(low_head, zero, vp)],
                                               axis=0))
                both = jnp.dot(wmix_ref[0, p], jnp.concatenate(rhs, axis=1), preferred_element_type=F32)
                mixed[0].append(both[:, :LANES])
                mixed[1].append(both[:, LANES:])
            for rows, parts in zip(blocks, mixed):
                u = _gelu_tanh(z_ref[rows, 0:aw])
                ycat_ref[rows, 0:aw] = (u * (jnp.concatenate(parts, axis=1) + bmix_ref[0])).astype(BF16)
        return run

    mixer = ([pool_stage(s) for s in range(geo.seqs_per_tile)]
             + [gate_stage(b) for b in range(0, tile // A_BLOCK, 2)])
    _run_interleaved(mixer, ffn)
    carry_ref[...] = z_ref[tile - POOL_PAD:tile, 2 * aw:]

    for g in range(len(POOL_WINDOWS)):
        lanes = slice(g * LANES, (g + 1) * LANES)
        yb = jnp.dot(pooled_ref[:, lanes].astype(BF16), pw_ref[g], preferred_element_type=F32)
        ycat_ref[:, aw + g * LANES:aw + (g + 1) * LANES] = (yb * bscale_ref[:, lanes]).astype(BF16)

    y = jnp.dot(ycat_ref[...], wout_ref[...], preferred_element_type=F32)
    x1_ref[...] = xres_ref[...] + y

    tail = slice(POOL_PAD + seg - POOL_HIST, POOL_PAD + seg)
    for s in range(geo.seqs_per_tile):
        pools_ref[s] = ext_ref[s, tail, :]
    poolp_ref[0] = ext_ref[geo.seqs_per_tile - 1, tail, :]
    _emit_rows(geo, step, final, acc_ref, gfin_ref, out_refs)


def _dwconv_rows(ext, first_row, w_ref, n_taps, r0, length, lanes):
    wlen = length + _round_up(first_row + n_taps - 1, SUBLANES)
    win = ext[r0:r0 + wlen, lanes]
    acc = None
    for r in range(SUBLANES):
        taps = [k for k in range(n_taps) if (first_row + k) % SUBLANES == r]
        if not taps:
            continue
        shifted = win if r == 0 else pltpu.roll(win, wlen - r, axis=0)
        for k in taps:
            q0 = (first_row + k) // SUBLANES * SUBLANES
            term = shifted[q0:q0 + length] * w_ref[k:k + 1, lanes]
            acc = term if acc is None else acc + term
    return acc


def _odd_layer_kernel(geo, n_x, final, n_cast, *refs):
    n_out = 2 if final else 1
    refs = _cast_next_weights(refs, n_x + 14, n_out + 4, n_cast)
    x_refs = refs[:n_x]
    (g_ref, win_ref, wout_ref, cw_ref, cb_ref, lng_ref, lnb_ref, dw_ref, histc_ref, histd_ref,
     gf_ref, w1_ref, w2_ref, gfin_ref) = refs[n_x:n_x + 14]
    out_refs = refs[n_x + 14:n_x + 14 + n_out]
    (cp_ref, dp_ref, cs_ref, ds_ref,
     x1_ref, acc_ref, z_ref, xres_ref, h_ref, t_ref, extc_ref, extd_ref, carryc_ref, carryd_ref, conv_ref,
     ycat_ref) = refs[n_x + 14 + n_out:]
    step, is_prompt, j = _tile_ids(geo)
    tile, seg = geo.tile, geo.sample_len
    cwid = wout_ref.shape[0] // 2
    kc, kd = cw_ref.shape[0], dw_ref.shape[0]
    hc, hd = kc - 1, kd - 1
    pc, pd = carryc_ref.shape[0], carryd_ref.shape[0]
    o = 2 * cwid

    @pl.when(step == 0)
    def _():
        x1_ref[...] = jnp.zeros(x1_ref.shape, F32)
        carryc_ref[...] = jnp.zeros(carryc_ref.shape, F32)
        carryd_ref[...] = jnp.zeros(carryd_ref.shape, F32)

    ffn = _ffn_stages(x1_ref, gf_ref, w1_ref, w2_ref, acc_ref, h_ref, t_ref, None if final else out_refs[0])
    _project(x_refs, is_prompt, g_ref, win_ref, xres_ref, z_ref)

    fresh = j == 0

    def conv_stage(s):
        def run(after=None):
            rows = slice(s * seg, (s + 1) * seg)
            c_in = z_ref[rows, 0:cwid] * jax.nn.sigmoid(z_ref[rows, cwid:2 * cwid])
            d_in = z_ref[rows, o + cwid:o + 2 * cwid] * z_ref[rows, o + 2 * cwid:o + 3 * cwid]
            if s == 0:
                prev_c = jnp.where(fresh, 0.0, carryc_ref[...])
                prev_d = jnp.where(fresh, 0.0, carryd_ref[...])
            else:
                prev_c = extc_ref[s - 1, seg:, :]
                prev_d = extd_ref[s - 1, seg:, :]
            extc_ref[s, 0:pc, :] = jnp.where(is_prompt, prev_c, histc_ref[s])
            extd_ref[s, 0:pd, :] = jnp.where(is_prompt, prev_d, histd_ref[s])
            extc_ref[s, pc:, :] = c_in
            extd_ref[s, pd:, :] = d_in
            extc, extd = extc_ref.at[s], extd_ref.at[s]
            if after is not None:
                _order_after(extc, after, range(cwid // LANES))
                _order_after(extd, after, range(cwid // LANES))
            cacc = None
            for lt in range(cwid // LANES):
                lanes = slice(lt * LANES, (lt + 1) * LANES)
                for r0 in range(0, seg, CONV_ROWS):
                    out_rows = slice(s * seg + r0, s * seg + r0 + CONV_ROWS)
                    cacc = _dwconv_rows(extc, pc - hc, cw_ref, kc, r0, CONV_ROWS, lanes)
                    conv_ref[out_rows, lanes] = cacc + cb_ref[:, lanes]
                    d = _dwconv_rows(extd, pd - hd, dw_ref, kd, r0, CONV_ROWS, lanes)
                    ycat_ref[out_rows, cwid + lt * LANES:cwid + (lt + 1) * LANES] = (
                        z_ref[out_rows, o + lt * LANES:o + (lt + 1) * LANES] * d).astype(BF16)
            return cacc[:SUBLANES]
        return run

    def norm_stage(b):
        def run():
            rows = slice(b * A_BLOCK, (b + 1) * A_BLOCK)
            c = _layer_norm(conv_ref[rows, :], lng_ref[...], lnb_ref[...])
            ycat_ref[rows, 0:cwid] = (c * jax.nn.sigmoid(c)).astype(BF16)
        return run

    assert len(ffn) == geo.seqs_per_tile
    tok_f, tok_c = None, None
    for k in range(len(ffn)):
        new_f = ffn[k](after=tok_c)
        tok_c = conv_stage(k)(after=tok_f)
        tok_f = new_f
    for b in range(tile // A_BLOCK):
        norm_stage(b)()
    last = geo.seqs_per_tile - 1
    carryc_ref[...] = extc_ref[last, seg:, :]
    carryd_ref[...] = extd_ref[last, seg:, :]

    y = jnp.dot(ycat_ref[...], wout_ref[...], preferred_element_type=F32)
    x1_ref[...] = xres_ref[...] + y

    tail_c, tail_d = slice(pc + seg - hc, pc + seg), slice(pd + seg - hd, pd + seg)
    for s in range(geo.seqs_per_tile):
        cs_ref[s] = extc_ref[s, tail_c, :]
        ds_ref[s] = extd_ref[s, tail_d, :]
    cp_ref[0] = extc_ref[last, tail_c, :]
    dp_ref[0] = extd_ref[last, tail_d, :]
    _emit_rows(geo, step, final, acc_ref, gfin_ref, out_refs)


def _layer_call(geo, name, body, xs, final, mixer_inputs, mixer_specs, ffn_inputs,
                mixer_out_shapes, mixer_out_specs, mixer_scratch, in_w, next_weights):
    d, tile, pt = geo.d_model, geo.tile, geo.prompt_tiles
    g_ffn, w1, w2, g_final = ffn_inputs
    assert geo.steps >= CAST_STEPS
    cast_in_specs, cast_specs, cast_shapes = [], [], []
    for w, index in next_weights:
        n_rows, cols = w.shape[1:]
        rows = n_rows // CAST_STEPS
        assert n_rows % CAST_STEPS == 0 and rows % (2 * SUBLANES) == 0, w.shape
        cast_in_specs.append(pl.BlockSpec((None, rows, cols),
                                          lambda s, index=index: (index, jnp.minimum(s, CAST_STEPS - 1), 0)))
        cast_specs.append(pl.BlockSpec((rows, cols), lambda s: (jnp.minimum(s, CAST_STEPS - 1), 0)))
        cast_shapes.append(jax.ShapeDtypeStruct((n_rows, cols), BF16))
    ffn_tile = geo.ffn_tile
    if len(xs) == 1:
        x_specs = [pl.BlockSpec((tile, d), lambda s: (geo.mix_tile(s), 0))]
    else:
        x_specs = [pl.BlockSpec((tile, d), lambda s: (jnp.minimum(s, pt - 1), 0)),
                   pl.BlockSpec((tile, d), lambda s: (jnp.clip(s - pt, 0, geo.sample_tiles - 1), 0))]
    ffn_specs = [_resident((1, d)), _resident(w1.shape), _resident(w2.shape), _resident((1, d))]
    if final:
        row_shapes = [jax.ShapeDtypeStruct((pt * tile, d), F32),
                      jax.ShapeDtypeStruct((geo.sample_tiles * tile, d), F32)]
        row_specs = [pl.BlockSpec((tile, d), lambda s: (jnp.minimum(ffn_tile(s), pt - 1), 0)),
                     pl.BlockSpec((tile, d), lambda s: (jnp.maximum(ffn_tile(s) - pt, 0), 0))]
    else:
        row_shapes = [jax.ShapeDtypeStruct((geo.tiles * tile, d), F32)]
        row_specs = [pl.BlockSpec((tile, d), lambda s: (ffn_tile(s), 0))]
    scratch = [((tile, d), F32),
               ((tile, d), F32),
               ((tile, in_w), F32),
               ((tile, d), F32),
               ((tile, d), BF16),
               ((tile, FF_CHUNK), BF16)] + mixer_scratch
    mixer_io = list(zip(mixer_specs, mixer_inputs)) + list(zip(mixer_out_specs, mixer_out_shapes))
    block_bytes = ((len(xs) + len(row_shapes)) * _nbytes((tile, d), F32)
                   + sum(_spec_bytes(sp, a.dtype) for sp, a in mixer_io if sp.pipeline_mode is None)
                   + sum(_spec_bytes(sp, F32) + _spec_bytes(sp, BF16) for sp in cast_specs))
    resident_bytes = (_nbytes(w1.shape, w1.dtype) + _nbytes(w2.shape, w2.dtype)
                      + sum(_spec_bytes(sp, a.dtype) for sp, a in mixer_io if sp.pipeline_mode is not None))
    scratch_bytes = sum(_nbytes(s, t) for s, t in scratch)
    temp_bytes = _nbytes((tile, FF_CHUNK), F32) + 2 * _nbytes((tile, d), F32)

    return pl.pallas_call(
        functools.partial(body, geo, len(xs), final, len(next_weights)),
        grid=(geo.steps,),
        in_specs=x_specs + mixer_specs + ffn_specs + cast_in_specs,
        out_specs=row_specs + mixer_out_specs + cast_specs,
        out_shape=row_shapes + mixer_out_shapes + cast_shapes,
        scratch_shapes=[pltpu.VMEM(s, t) for s, t in scratch],
        compiler_params=pltpu.CompilerParams(
            dimension_semantics=("arbitrary",),
            vmem_limit_bytes=_vmem_limit(block_bytes, resident_bytes, scratch_bytes, temp_bytes)),
        name=f"{name[0]}_layer_{name[1]}",
    )(*xs, *mixer_inputs, g_ffn, w1, w2, g_final, *(w for w, _ in next_weights))


def _segment_maps(geo):
    pt = geo.prompt_tiles

    def sample_tile(s):
        return jnp.maximum(geo.mix_tile(s) - pt, 0)

    def prompt_seq(s):
        i = geo.mix_tile(s)
        return jnp.where(i < pt, i // geo.tiles_per_seq, geo.n_prompt_seq)

    return sample_tile, prompt_seq


def _even_layer(geo, layer, xs, final, g_mix, w_in, w_out, wmix, bmix, ln_g, ln_b, pool_w, b_scale,
                hist_pool, ffn_inputs, next_weights):
    li = layer // 2
    tile, spt, seg, pt = geo.tile, geo.seqs_per_tile, geo.sample_len, geo.prompt_tiles
    in_w = w_in.shape[1]
    aw = w_out.shape[0] // 2
    bw = in_w - 2 * aw
    n_sample_seq = hist_pool.shape[1]
    sample_tile, prompt_seq = _segment_maps(geo)

    mixer_specs = [
        _resident((1, geo.d_model)),
        _resident(w_in.shape),
        _resident(w_out.shape),
        pl.BlockSpec((1,) + wmix.shape[1:], lambda s: (geo.mix_tile(s) // pt, 0, 0, 0)),
        pl.BlockSpec((1,) + bmix.shape[1:], lambda s: (geo.mix_tile(s) // pt, 0, 0)),
        _resident((1, aw)),
        _resident((1, aw)),
        _resident(pool_w.shape),
        _resident((1, bw)),
        pl.BlockSpec((None, spt, POOL_PAD, bw), lambda s: (li, sample_tile(s), 0, 0)),
    ]
    out_shapes = [
        jax.ShapeDtypeStruct((geo.sample_tiles * tile, aw), F32),
        jax.ShapeDtypeStruct((geo.n_prompt_seq + 1, POOL_HIST, bw), F32),
        jax.ShapeDtypeStruct((n_sample_seq, POOL_HIST, bw), F32),
    ]
    out_specs = [
        pl.BlockSpec((tile, aw), lambda s: (sample_tile(s), 0)),
        pl.BlockSpec((1, POOL_HIST, bw), lambda s: (prompt_seq(s), 0, 0)),
        pl.BlockSpec((spt, POOL_HIST, bw), lambda s: (sample_tile(s), 0, 0)),
    ]
    scratch = [
        ((spt, POOL_PAD + seg, bw), F32),
        ((POOL_PAD, bw), F32),
        ((tile, bw), F32),
        ((tile, aw + bw), BF16),
    ]
    outs = _layer_call(
        geo, ("even", layer), _even_layer_kernel, xs, final,
        (g_mix, w_in, w_out, wmix, bmix, ln_g, ln_b, pool_w, b_scale, hist_pool), mixer_specs, ffn_inputs,
        out_shapes, out_specs, scratch, in_w, next_weights)
    n_rows = 2 if final else 1
    v_s, pool_p, pool_s = outs[n_rows:n_rows + 3]
    return outs[:n_rows], v_s, pool_p[:geo.n_prompt_seq], pool_s, outs[n_rows + 3:]


def _odd_layer(geo, layer, xs, final, g_mix, w_in, w_out, conv_w, conv_b, ln_g, ln_b, dconv_w,
               hist_c, hist_d, ffn_inputs, next_weights):
    li = layer // 2
    tile, spt, seg = geo.tile, geo.seqs_per_tile, geo.sample_len
    in_w = w_in.shape[1]
    cwid = w_out.shape[0] // 2
    kc, kd = conv_w.shape[0], dconv_w.shape[0]
    hc, hd = kc - 1, kd - 1
    pc, pd = hist_c.shape[2], hist_d.shape[2]
    n_sample_seq = hist_c.shape[1]
    sample_tile, prompt_seq = _segment_maps(geo)

    mixer_specs = [
        _resident((1, geo.d_model)),
        _resident(w_in.shape),
        _resident(w_out.shape),
        _resident(conv_w.shape),
        _resident((1, cwid)),
        _resident((1, cwid)),
        _resident((1, cwid)),
        _resident(dconv_w.shape),
        pl.BlockSpec((None, spt, pc, cwid), lambda s: (li, sample_tile(s), 0, 0)),
        pl.BlockSpec((None, spt, pd, cwid), lambda s: (li, sample_tile(s), 0, 0)),
    ]
    out_shapes = [
        jax.ShapeDtypeStruct((geo.n_prompt_seq + 1, hc, cwid), F32),
        jax.ShapeDtypeStruct((geo.n_prompt_seq + 1, hd, cwid), F32),
        jax.ShapeDtypeStruct((n_sample_seq, hc, cwid), F32),
        jax.ShapeDtypeStruct((n_sample_seq, hd, cwid), F32),
    ]
    out_specs = [
        pl.BlockSpec((1, hc, cwid), lambda s: (prompt_seq(s), 0, 0)),
        pl.BlockSpec((1, hd, cwid), lambda s: (prompt_seq(s), 0, 0)),
        pl.BlockSpec((spt, hc, cwid), lambda s: (sample_tile(s), 0, 0)),
        pl.BlockSpec((spt, hd, cwid), lambda s: (sample_tile(s), 0, 0)),
    ]
    scratch = [
        ((spt, pc + seg, cwid), F32),
        ((spt, pd + seg, cwid), F32),
        ((pc, cwid), F32),
        ((pd, cwid), F32),
        ((tile, cwid), F32),
        ((tile, 2 * cwid), BF16),
    ]
    outs = _layer_call(
        geo, ("odd", layer), _odd_layer_kernel, xs, final,
        (g_mix, w_in, w_out, conv_w, conv_b, ln_g, ln_b, dconv_w, hist_c, hist_d), mixer_specs, ffn_inputs,
        out_shapes, out_specs, scratch, in_w, next_weights)
    n_rows = 2 if final else 1
    c_p, d_p, c_s, d_s = outs[n_rows:n_rows + 4]
    return outs[:n_rows], c_p[:geo.n_prompt_seq], d_p[:geo.n_prompt_seq], c_s, d_s, outs[n_rows + 4:]


def _gating_operands(w_s, b_s, sample_len):
    heads, blk, _ = w_s.shape
    cidx = jnp.arange(blk) // CHUNK
    mask = (cidx[None, :] <= cidx[:, None]).astype(w_s.dtype)
    w_prompt = w_s * mask[None]
    head_dim_lanes = LANES // 2
    reps = blk // sample_len
    eye = jnp.eye(reps, dtype=w_s.dtype)
    w_first = w_prompt[:, :sample_len, :sample_len]
    w_sample = jnp.einsum("ab,hij->haibj", eye, w_first).reshape(heads, blk, blk)
    b_sample = jnp.tile(b_s[:, :sample_len], (1, reps))

    def pairs(w):
        return jnp.concatenate([w[0::2], w[1::2]], axis=2)

    def bias(b):
        return jnp.repeat(b.T, head_dim_lanes, axis=1)

    wmix = jnp.stack([pairs(w_prompt), pairs(w_sample)]).astype(BF16)
    bmix = jnp.stack([bias(b_s), bias(b_sample)]).astype(F32)
    return wmix, bmix


def _pad_history(state):
    rows = state.shape[2]
    return jnp.pad(state, ((0, 0), (0, 0), (_round_up(rows, SUBLANES) - rows, 0), (0, 0)))


def kernel(x_prompt, x_sample, state_pool, state_conv_c, state_conv_d, norm_mix_g, norm_ffn_g, final_norm_g,
           w_in_even, w_out_even, a_w_s, a_b_s, a_ln_g, a_ln_b, b_pool_w, b_scale,
           w_in_odd, w_out_odd, c_conv_w, c_conv_b, c_ln_g, c_ln_b, d_conv_w, w_ff1, w_ff2):
    n_prompt_seq, seq, d = x_prompt.shape
    n_sample_seq, sample_len, _ = x_sample.shape
    depth = norm_mix_g.shape[0]
    tile = ROW_TILE
    assert seq % tile == 0 and tile % A_BLOCK == 0 and tile % sample_len == 0
    assert A_BLOCK % sample_len == 0 and sample_len <= CHUNK
    assert (n_sample_seq * sample_len) % tile == 0
    assert a_w_s.shape[1] == A_HEADS and a_w_s.shape[2] == A_BLOCK
    assert w_out_even.shape[1] // 2 == A_HEADS * (LANES // 2)
    assert state_pool.shape[2] == POOL_HIST and sample_len >= _round_up(c_conv_w.shape[1] - 1, SUBLANES)
    geo = Geometry(
        d_model=d, tile=tile,
        prompt_tiles=n_prompt_seq * seq // tile, tiles_per_seq=seq // tile, n_prompt_seq=n_prompt_seq,
        sample_tiles=n_sample_seq * sample_len // tile, sample_len=sample_len,
        seqs_per_tile=tile // sample_len)

    def row(v):
        return v.reshape(1, -1)

    pool_w_b = b_pool_w.astype(BF16)
    hist_pool, hist_c, hist_d = _pad_history(state_pool), _pad_history(state_conv_c), _pad_history(state_conv_d)

    def layer_weights(layer):
        li = layer // 2
        w_in, w_out = (w_in_even, w_out_even) if layer % 2 == 0 else (w_in_odd, w_out_odd)
        return (w_in, li), (w_out, li), (w_ff1, layer), (w_ff2, layer)

    weights_b = [w[index].astype(BF16) for w, index in layer_weights(0)]
    xs = (x_prompt.reshape(-1, d), x_sample.reshape(-1, d))
    pools_p, pools_s, vs, cs_p, cs_s, ds_p, ds_s = [], [], [], [], [], [], []
    for layer in range(depth):
        li = layer // 2
        final = layer == depth - 1
        w_in_b, w_out_b, w1_b, w2_b = weights_b
        next_weights = () if final else layer_weights(layer + 1)
        ffn_inputs = (row(norm_ffn_g[layer]), w1_b, w2_b, row(final_norm_g))
        if layer % 2 == 0:
            wmix, bmix = _gating_operands(a_w_s[li], a_b_s[li], sample_len)
            xs, v_s, pool_p, pool_s, weights_b = _even_layer(
                geo, layer, xs, final, row(norm_mix_g[layer]), w_in_b, w_out_b, wmix, bmix,
                row(a_ln_g[li]), row(a_ln_b[li]), pool_w_b[li], row(b_scale[li]), hist_pool, ffn_inputs,
                next_weights)
            vs.append(v_s.reshape(n_sample_seq, sample_len, -1))
            pools_p.append(pool_p)
            pools_s.append(pool_s)
        else:
            xs, c_p, d_p, c_s, d_s, weights_b = _odd_layer(
                geo, layer, xs, final, row(norm_mix_g[layer]), w_in_b, w_out_b, c_conv_w[li],
                row(c_conv_b[li]), row(c_ln_g[li]), row(c_ln_b[li]), d_conv_w[li], hist_c, hist_d, ffn_inputs,
                next_weights)
            cs_p.append(c_p)
            ds_p.append(d_p)
            cs_s.append(c_s)
            ds_s.append(d_s)

    y_prompt = xs[0].reshape(n_prompt_seq, seq, d)
    y_sample = xs[1].reshape(n_sample_seq, sample_len, d)
    return (y_prompt, y_sample, jnp.stack(pools_p), jnp.stack(cs_p), jnp.stack(ds_p),
            jnp.stack(vs), jnp.stack(pools_s), jnp.stack(cs_s), jnp.stack(ds_s))
```

```python
import functools
from typing import NamedTuple

import jax
import jax.numpy as jnp
from jax import lax
from jax.experimental import pallas as pl
from jax.experimental.pallas import tpu as pltpu

F32 = jnp.float32
BF16 = jnp.bfloat16

EPS = 1e-6
PAST_LEN = 4096
CHUNK = 64
A_BLOCK = 128
A_HEADS = 8
POOL_WINDOWS = (2, 4, 8, 16)
LANES = 128
SUBLANES = 8
VMEM_CAPACITY_BYTES = 64 * 1024 * 1024
ROW_TILE = 512
FF_CHUNK = 1024
CONV_ROWS = 64
CAST_STEPS = 32

POOL_HIST = max(POOL_WINDOWS) - 1


def _round_up(n, m):
    return -(-n // m) * m


POOL_PAD = _round_up(POOL_HIST, SUBLANES)


class Geometry(NamedTuple):
    d_model: int
    tile: int
    prompt_tiles: int
    tiles_per_seq: int
    n_prompt_seq: int
    sample_tiles: int
    sample_len: int
    seqs_per_tile: int

    @property
    def tiles(self):
        return self.prompt_tiles + self.sample_tiles

    @property
    def steps(self):
        return self.tiles + 1

    def mix_tile(self, s):
        return jnp.minimum(s, self.tiles - 1)

    def ffn_tile(self, s):
        return jnp.maximum(s - 1, 0)


def _resident(shape, index=None):
    full = (0,) * len(shape) if index is None else tuple(index)
    return pl.BlockSpec(shape, lambda s: full, pipeline_mode=pl.Buffered(1))


def _spec_bytes(spec, dtype):
    return _nbytes([b for b in spec.block_shape if b is not None], dtype)


def _nbytes(shape, dtype):
    n = 1
    for s in shape:
        n *= s
    return n * jnp.dtype(dtype).itemsize


def _vmem_limit(block_bytes, resident_bytes, scratch_bytes, temp_bytes):
    need = 2 * block_bytes + resident_bytes + scratch_bytes + temp_bytes
    assert need <= VMEM_CAPACITY_BYTES, need
    return min(VMEM_CAPACITY_BYTES - (2 << 20), _round_up(need + need // 8, 1 << 20))


def _rms_norm(x, g):
    y = x * lax.rsqrt(jnp.mean(x * x, axis=-1, keepdims=True) + EPS)
    return y * g


def _layer_norm(x, g, b):
    mu = jnp.mean(x, axis=-1, keepdims=True)
    xc = x - mu
    y = xc * lax.rsqrt(jnp.mean(xc * xc, axis=-1, keepdims=True) + EPS)
    return y * g + b


def _gelu_tanh(x):
    c = 0.7978845608028654
    return x * (0.5 * (1.0 + jnp.tanh(c * (x + 0.044715 * (x * x * x)))))


def _load_rows(x_refs, is_prompt):
    if len(x_refs) == 1:
        return x_refs[0][...]
    return jnp.where(is_prompt, x_refs[0][...], x_refs[1][...])


def _tile_ids(geo):
    step = pl.program_id(0)
    m = geo.mix_tile(step)
    return step, m < geo.prompt_tiles, m % geo.tiles_per_seq


def _derived_zero(x):
    u = pltpu.bitcast(x, jnp.uint32)
    u = lax.shift_right_logical(lax.shift_right_logical(u, jnp.uint32(16)), jnp.uint32(16))
    return u.astype(F32)


def _order_after(ref, token, lane_tiles=(0,)):
    rows = SUBLANES * (4 // jnp.dtype(ref.dtype).itemsize)
    z = _derived_zero(token)
    z = jnp.concatenate([z] * (rows // SUBLANES), axis=0).astype(ref.dtype)
    for lt in lane_tiles:
        idx = (slice(0, rows), slice(lt * LANES, (lt + 1) * LANES))
        ref[idx] = ref[idx] + z


def _ffn_stages(x1_ref, g_ref, w1_ref, w2_ref, acc_ref, h_ref, t_ref, dst_ref=None):
    n_chunks = w1_ref.shape[1] // FF_CHUNK

    def up(c):
        def run(after=None):
            if c == 0:
                x = x1_ref[...]
                h_ref[...] = _rms_norm(x, g_ref[...]).astype(BF16)
                acc_ref[...] = x
            if after is not None:
                _order_after(h_ref, after)
            cols = slice(c * FF_CHUNK, (c + 1) * FF_CHUNK)
            t = jnp.maximum(jnp.dot(h_ref[...], w1_ref[:, cols], preferred_element_type=F32), 0.0)
            t_ref[...] = (t * t).astype(BF16)
            return t[-SUBLANES:, -LANES:]
        return run

    def down(c):
        def run(after=None):
            if after is not None:
                _order_after(t_ref, after)
            cols = slice(c * FF_CHUNK, (c + 1) * FF_CHUNK)
            d = jnp.dot(t_ref[...], w2_ref[cols, :], preferred_element_type=F32)
            if dst_ref is not None and c == n_chunks - 1:
                dst_ref[...] = acc_ref[...] + d
            else:
                acc_ref[...] += d
            return d[-SUBLANES:, -LANES:]
        return run

    return [stage(c) for c in range(n_chunks) for stage in (up, down)]


def _run_interleaved(*stage_lists):
    order = sorted(((k + 0.5) / len(stages), n, k) for n, stages in enumerate(stage_lists) for k in range(len(stages)))
    for _, n, k in order:
        stage_lists[n][k]()


def _project(x_refs, is_prompt, g_ref, win_ref, z_ref):
    x = _load_rows(x_refs, is_prompt)
    z_ref[...] = jnp.dot(_rms_norm(x, g_ref[...]).astype(BF16), win_ref[...], preferred_element_type=F32)


def _emit_rows(geo, step, final, acc_ref, gfin_ref, out_refs):
    if not final:
        return
    is_prompt = geo.ffn_tile(step) < geo.prompt_tiles

    @pl.when(is_prompt)
    def _():
        out_refs[0][...] = _rms_norm(acc_ref[...], gfin_ref[...])

    @pl.when(jnp.logical_not(is_prompt))
    def _():
        out_refs[1][...] = _rms_norm(acc_ref[...], gfin_ref[...])


def _pool_segment(ext, length, pos0, out_ref, out_base):
    pos = pos0 + lax.broadcasted_iota(jnp.int32, (length, 1), 0)
    for g, w in enumerate(POOL_WINDOWS):
        lanes = slice(g * LANES, (g + 1) * LANES)
        win = ext[0:POOL_PAD + length, lanes]
        s = win
        span = 1
        while span < w:
            s = s + pltpu.roll(s, span, axis=0)
            span *= 2
        inv_cnt = 1.0 / jnp.minimum(w, pos + 1).astype(F32)
        out_ref[out_base:out_base + length, lanes] = s[POOL_PAD:] * inv_cnt - win[POOL_PAD:]


def _cast_next_weights(refs, n_in, n_outs, n_cast):
    src = refs[n_in:n_in + n_cast]
    dst = refs[n_in + n_cast + n_outs:n_in + n_cast + n_outs + n_cast]
    for i_ref, o_ref in zip(src, dst):
        o_ref[...] = i_ref[...].astype(BF16)
    return refs[:n_in] + refs[n_in + n_cast:n_in + n_cast + n_outs] + refs[n_in + 2 * n_cast + n_outs:]


def _even_layer_kernel(geo, n_x, final, n_cast, *refs):
    n_out = 2 if final else 1
    refs = _cast_next_weights(refs, n_x + 14, n_out + 3, n_cast)
    x_refs = refs[:n_x]
    (g_ref, win_ref, wout_ref, wmix_ref, bmix_ref, lng_ref, lnb_ref, pw_ref, bscale_ref, hist_ref,
     gf_ref, w1_ref, w2_ref, gfin_ref) = refs[n_x:n_x + 14]
    out_refs = refs[n_x + 14:n_x + 14 + n_out]
    (v_ref, poolp_ref, pools_ref,
     x1_ref, acc_ref, z_ref, h_ref, t_ref, ext_ref, carry_ref, pooled_ref,
     ycat_ref) = refs[n_x + 14 + n_out:]
    step, is_prompt, j = _tile_ids(geo)
    tile, seg = geo.tile, geo.sample_len
    aw = wout_ref.shape[0] // 2

    @pl.when(step == 0)
    def _():
        x1_ref[...] = jnp.zeros(x1_ref.shape, F32)
        carry_ref[...] = jnp.zeros(carry_ref.shape, F32)

    ffn = _ffn_stages(x1_ref, gf_ref, w1_ref, w2_ref, acc_ref, h_ref, t_ref, None if final else out_refs[0])
    _project(x_refs, is_prompt, g_ref, win_ref, z_ref)

    fresh = j == 0

    def pool_stage(s):
        def run():
            rows = slice(s * seg, (s + 1) * seg)
            if s == 0:
                before = jnp.where(fresh, 0.0, carry_ref[...])
            else:
                before = z_ref[s * seg - POOL_PAD:s * seg, 2 * aw:]
            ext_ref[s, 0:POOL_PAD, :] = jnp.where(is_prompt, before, hist_ref[s])
            ext_ref[s, POOL_PAD:, :] = z_ref[rows, 2 * aw:]
            pos0 = jnp.where(is_prompt, j * tile + s * seg, PAST_LEN)
            _pool_segment(ext_ref.at[s], seg, pos0, pooled_ref, s * seg)
        return run

    lane = lax.broadcasted_iota(jnp.int32, (A_BLOCK, LANES), 1)
    low_head = lane < (LANES // 2)
    n_pairs = aw // LANES

    def gate_stage(b):
        def run():
            blocks = (slice(b * A_BLOCK, (b + 1) * A_BLOCK), slice((b + 1) * A_BLOCK, (b + 2) * A_BLOCK))
            vs = []
            for rows in blocks:
                v = _layer_norm(_gelu_tanh(z_ref[rows, aw:2 * aw]), lng_ref[...], lnb_ref[...])
                v_ref[rows, :] = v
                vs.append(v)
            mixed = ([], [])
            for p in range(n_pairs):
                rhs = []
                for v in vs:
                    vp = v[:, p * LANES:(p + 1) * LANES].astype(BF16)
                    zero = jnp.zeros_like(vp)
                    rhs.append(jnp.concatenate([jnp.where(low_head, vp, zero), jnp.where(low_head, zero, vp)],
                                               axis=0))
                both = jnp.dot(wmix_ref[0, p], jnp.concatenate(rhs, axis=1), preferred_element_type=F32)
                mixed[0].append(both[:, :LANES])
                mixed[1].append(both[:, LANES:])
            for rows, parts in zip(blocks, mixed):
                u = _gelu_tanh(z_ref[rows, 0:aw])
                ycat_ref[rows, 0:aw] = (u * (jnp.concatenate(parts, axis=1) + bmix_ref[0])).astype(BF16)
        return run

    mixer = ([pool_stage(s) for s in range(geo.seqs_per_tile)]
             + [gate_stage(b) for b in range(0, tile // A_BLOCK, 2)])
    _run_interleaved(mixer, ffn)
    carry_ref[...] = z_ref[tile - POOL_PAD:tile, 2 * aw:]

    for g in range(len(POOL_WINDOWS)):
        lanes = slice(g * LANES, (g + 1) * LANES)
        yb = jnp.dot(pooled_ref[:, lanes].astype(BF16), pw_ref[g], preferred_element_type=F32)
        ycat_ref[:, aw + g * LANES:aw + (g + 1) * LANES] = (yb * bscale_ref[:, lanes]).astype(BF16)

    y = jnp.dot(ycat_ref[...], wout_ref[...], preferred_element_type=F32)
    x1_ref[...] = _load_rows(x_refs, is_prompt) + y

    tail = slice(POOL_PAD + seg - POOL_HIST, POOL_PAD + seg)
    for s in range(geo.seqs_per_tile):
        pools_ref[s] = ext_ref[s, tail, :]
    poolp_ref[0] = ext_ref[geo.seqs_per_tile - 1, tail, :]
    _emit_rows(geo, step, final, acc_ref, gfin_ref, out_refs)


def _dwconv_rows(ext, first_row, w_ref, n_taps, r0, length, lanes):
    wlen = length + _round_up(first_row + n_taps - 1, SUBLANES)
    win = ext[r0:r0 + wlen, lanes]
    acc = None
    for r in range(SUBLANES):
        taps = [k for k in range(n_taps) if (first_row + k) % SUBLANES == r]
        if not taps:
            continue
        shifted = win if r == 0 else pltpu.roll(win, wlen - r, axis=0)
        for k in taps:
            q0 = (first_row + k) // SUBLANES * SUBLANES
            term = shifted[q0:q0 + length] * w_ref[k:k + 1, lanes]
            acc = term if acc is None else acc + term
    return acc


def _odd_layer_kernel(geo, n_x, final, n_cast, *refs):
    n_out = 2 if final else 1
    refs = _cast_next_weights(refs, n_x + 14, n_out + 4, n_cast)
    x_refs = refs[:n_x]
    (g_ref, win_ref, wout_ref, cw_ref, cb_ref, lng_ref, lnb_ref, dw_ref, histc_ref, histd_ref,
     gf_ref, w1_ref, w2_ref, gfin_ref) = refs[n_x:n_x + 14]
    out_refs = refs[n_x + 14:n_x + 14 + n_out]
    (cp_ref, dp_ref, cs_ref, ds_ref,
     x1_ref, acc_ref, z_ref, h_ref, t_ref, extc_ref, extd_ref, carryc_ref, carryd_ref, conv_ref,
     ycat_ref) = refs[n_x + 14 + n_out:]
    step, is_prompt, j = _tile_ids(geo)
    tile, seg = geo.tile, geo.sample_len
    cwid = wout_ref.shape[0] // 2
    kc, kd = cw_ref.shape[0], dw_ref.shape[0]
    hc, hd = kc - 1, kd - 1
    pc, pd = carryc_ref.shape[0], carryd_ref.shape[0]
    o = 2 * cwid

    @pl.when(step == 0)
    def _():
        x1_ref[...] = jnp.zeros(x1_ref.shape, F32)
        carryc_ref[...] = jnp.zeros(carryc_ref.shape, F32)
        carryd_ref[...] = jnp.zeros(carryd_ref.shape, F32)

    ffn = _ffn_stages(x1_ref, gf_ref, w1_ref, w2_ref, acc_ref, h_ref, t_ref, None if final else out_refs[0])
    _project(x_refs, is_prompt, g_ref, win_ref, z_ref)

    fresh = j == 0

    def conv_stage(s):
        def run(after=None):
            rows = slice(s * seg, (s + 1) * seg)
            c_in = z_ref[rows, 0:cwid] * jax.nn.sigmoid(z_ref[rows, cwid:2 * cwid])
            d_in = z_ref[rows, o + cwid:o + 2 * cwid] * z_ref[rows, o + 2 * cwid:o + 3 * cwid]
            if s == 0:
                prev_c = jnp.where(fresh, 0.0, carryc_ref[...])
                prev_d = jnp.where(fresh, 0.0, carryd_ref[...])
            else:
                prev_c = extc_ref[s - 1, seg:, :]
                prev_d = extd_ref[s - 1, seg:, :]
            extc_ref[s, 0:pc, :] = jnp.where(is_prompt, prev_c, histc_ref[s])
            extd_ref[s, 0:pd, :] = jnp.where(is_prompt, prev_d, histd_ref[s])
            extc_ref[s, pc:, :] = c_in
            extd_ref[s, pd:, :] = d_in
            extc, extd = extc_ref.at[s], extd_ref.at[s]
            if after is not None:
                _order_after(extc, after, range(cwid // LANES))
                _order_after(extd, after, range(cwid // LANES))
            cacc = None
            for lt in range(cwid // LANES):
                lanes = slice(lt * LANES, (lt + 1) * LANES)
                for r0 in range(0, seg, CONV_ROWS):
                    out_rows = slice(s * seg + r0, s * seg + r0 + CONV_ROWS)
                    cacc = _dwconv_rows(extc, pc - hc, cw_ref, kc, r0, CONV_ROWS, lanes)
                    conv_ref[out_rows, lanes] = cacc + cb_ref[:, lanes]
                    d = _dwconv_rows(extd, pd - hd, dw_ref, kd, r0, CONV_ROWS, lanes)
                    ycat_ref[out_rows, cwid + lt * LANES:cwid + (lt + 1) * LANES] = (
                        z_ref[out_rows, o + lt * LANES:o + (lt + 1) * LANES] * d).astype(BF16)
            return cacc[:SUBLANES]
        return run

    def norm_stage(b):
        def run():
            rows = slice(b * A_BLOCK, (b + 1) * A_BLOCK)
            c = _layer_norm(conv_ref[rows, :], lng_ref[...], lnb_ref[...])
            ycat_ref[rows, 0:cwid] = (c * jax.nn.sigmoid(c)).astype(BF16)
        return run

    assert len(ffn) == geo.seqs_per_tile
    tok_f, tok_c = None, None
    for k in range(len(ffn)):
        new_f = ffn[k](after=tok_c)
        tok_c = conv_stage(k)(after=tok_f)
        tok_f = new_f
    for b in range(tile // A_BLOCK):
        norm_stage(b)()
    last = geo.seqs_per_tile - 1
    carryc_ref[...] = extc_ref[last, seg:, :]
    carryd_ref[...] = extd_ref[last, seg:, :]

    y = jnp.dot(ycat_ref[...], wout_ref[...], preferred_element_type=F32)
    x1_ref[...] = _load_rows(x_refs, is_prompt) + y

    tail_c, tail_d = slice(pc + seg - hc, pc + seg), slice(pd + seg - hd, pd + seg)
    for s in range(geo.seqs_per_tile):
        cs_ref[s] = extc_ref[s, tail_c, :]
        ds_ref[s] = extd_ref[s, tail_d, :]
    cp_ref[0] = extc_ref[last, tail_c, :]
    dp_ref[0] = extd_ref[last, tail_d, :]
    _emit_rows(geo, step, final, acc_ref, gfin_ref, out_refs)


def _layer_call(geo, name, body, xs, final, mixer_inputs, mixer_specs, ffn_inputs,
                mixer_out_shapes, mixer_out_specs, mixer_scratch, in_w, next_weights):
    d, tile, pt = geo.d_model, geo.tile, geo.prompt_tiles
    g_ffn, w1, w2, g_final = ffn_inputs
    assert geo.steps >= CAST_STEPS
    cast_in_specs, cast_specs, cast_shapes = [], [], []
    for w, index in next_weights:
        n_rows, cols = w.shape[1:]
        rows = n_rows // CAST_STEPS
        assert n_rows % CAST_STEPS == 0 and rows % (2 * SUBLANES) == 0, w.shape
        cast_in_specs.append(pl.BlockSpec((None, rows, cols),
                                          lambda s, index=index: (index, jnp.minimum(s, CAST_STEPS - 1), 0)))
        cast_specs.append(pl.BlockSpec((rows, cols), lambda s: (jnp.minimum(s, CAST_STEPS - 1), 0)))
        cast_shapes.append(jax.ShapeDtypeStruct((n_rows, cols), BF16))
    ffn_tile = geo.ffn_tile
    if len(xs) == 1:
        x_specs = [pl.BlockSpec((tile, d), lambda s: (geo.mix_tile(s), 0))]
    else:
        x_specs = [pl.BlockSpec((tile, d), lambda s: (jnp.minimum(s, pt - 1), 0)),
                   pl.BlockSpec((tile, d), lambda s: (jnp.clip(s - pt, 0, geo.sample_tiles - 1), 0))]
    ffn_specs = [_resident((1, d)), _resident(w1.shape), _resident(w2.shape), _resident((1, d))]
    if final:
        row_shapes = [jax.ShapeDtypeStruct((pt * tile, d), F32),
                      jax.ShapeDtypeStruct((geo.sample_tiles * tile, d), F32)]
        row_specs = [pl.BlockSpec((tile, d), lambda s: (jnp.minimum(ffn_tile(s), pt - 1), 0)),
                     pl.BlockSpec((tile, d), lambda s: (jnp.maximum(ffn_tile(s) - pt, 0), 0))]
    else:
        row_shapes = [jax.ShapeDtypeStruct((geo.tiles * tile, d), F32)]
        row_specs = [pl.BlockSpec((tile, d), lambda s: (ffn_tile(s), 0))]
    scratch = [((tile, d), F32),
               ((tile, d), F32),
               ((tile, in_w), F32),
               ((tile, d), BF16),
               ((tile, FF_CHUNK), BF16)] + mixer_scratch
    mixer_io = list(zip(mixer_specs, mixer_inputs)) + list(zip(mixer_out_specs, mixer_out_shapes))
    block_bytes = ((len(xs) + len(row_shapes)) * _nbytes((tile, d), F32)
                   + sum(_spec_bytes(sp, a.dtype) for sp, a in mixer_io if sp.pipeline_mode is None)
                   + sum(_spec_bytes(sp, F32) + _spec_bytes(sp, BF16) for sp in cast_specs))
    resident_bytes = (_nbytes(w1.shape, w1.dtype) + _nbytes(w2.shape, w2.dtype)
                      + sum(_spec_bytes(sp, a.dtype) for sp, a in mixer_io if sp.pipeline_mode is not None))
    scratch_bytes = sum(_nbytes(s, t) for s, t in scratch)
    temp_bytes = _nbytes((tile, FF_CHUNK), F32) + 2 * _nbytes((tile, d), F32)

    return pl.pallas_call(
        functools.partial(body, geo, len(xs), final, len(next_weights)),
        grid=(geo.steps,),
        in_specs=x_specs + mixer_specs + ffn_specs + cast_in_specs,
        out_specs=row_specs + mixer_out_specs + cast_specs,
        out_shape=row_shapes + mixer_out_shapes + cast_shapes,
        scratch_shapes=[pltpu.VMEM(s, t) for s, t in scratch],
        compiler_params=pltpu.CompilerParams(
            dimension_semantics=("arbitrary",),
            vmem_limit_bytes=_vmem_limit(block_bytes, resident_bytes, scratch_bytes, temp_bytes)),
        name=f"{name[0]}_layer_{name[1]}",
    )(*xs, *mixer_inputs, g_ffn, w1, w2, g_final, *(w for w, _ in next_weights))


def _segment_maps(geo):
    pt = geo.prompt_tiles

    def sample_tile(s):
        return jnp.maximum(geo.mix_tile(s) - pt, 0)

    def prompt_seq(s):
        i = geo.mix_tile(s)
        return jnp.where(i < pt, i // geo.tiles_per_seq, geo.n_prompt_seq)

    return sample_tile, prompt_seq


def _even_layer(geo, layer, xs, final, g_mix, w_in, w_out, wmix, bmix, ln_g, ln_b, pool_w, b_scale,
                hist_pool, ffn_inputs, next_weights):
    li = layer // 2
    tile, spt, seg, pt = geo.tile, geo.seqs_per_tile, geo.sample_len, geo.prompt_tiles
    in_w = w_in.shape[1]
    aw = w_out.shape[0] // 2
    bw = in_w - 2 * aw
    n_sample_seq = hist_pool.shape[1]
    sample_tile, prompt_seq = _segment_maps(geo)

    mixer_specs = [
        _resident((1, geo.d_model)),
        _resident(w_in.shape),
        _resident(w_out.shape),
        pl.BlockSpec((1,) + wmix.shape[1:], lambda s: (geo.mix_tile(s) // pt, 0, 0, 0)),
        pl.BlockSpec((1,) + bmix.shape[1:], lambda s: (geo.mix_tile(s) // pt, 0, 0)),
        _resident((1, aw)),
        _resident((1, aw)),
        _resident(pool_w.shape),
        _resident((1, bw)),
        pl.BlockSpec((None, spt, POOL_PAD, bw), lambda s: (li, sample_tile(s), 0, 0)),
    ]
    out_shapes = [
        jax.ShapeDtypeStruct((geo.sample_tiles * tile, aw), F32),
        jax.ShapeDtypeStruct((geo.n_prompt_seq + 1, POOL_HIST, bw), F32),
        jax.ShapeDtypeStruct((n_sample_seq, POOL_HIST, bw), F32),
    ]
    out_specs = [
        pl.BlockSpec((tile, aw), lambda s: (sample_tile(s), 0)),
        pl.BlockSpec((1, POOL_HIST, bw), lambda s: (prompt_seq(s), 0, 0)),
        pl.BlockSpec((spt, POOL_HIST, bw), lambda s: (sample_tile(s), 0, 0)),
    ]
    scratch = [
        ((spt, POOL_PAD + seg, bw), F32),
        ((POOL_PAD, bw), F32),
        ((tile, bw), F32),
        ((tile, aw + bw), BF16),
    ]
    outs = _layer_call(
        geo, ("even", layer), _even_layer_kernel, xs, final,
        (g_mix, w_in, w_out, wmix, bmix, ln_g, ln_b, pool_w, b_scale, hist_pool), mixer_specs, ffn_inputs,
        out_shapes, out_specs, scratch, in_w, next_weights)
    n_rows = 2 if final else 1
    v_s, pool_p, pool_s = outs[n_rows:n_rows + 3]
    return outs[:n_rows], v_s, pool_p[:geo.n_prompt_seq], pool_s, outs[n_rows + 3:]


def _odd_layer(geo, layer, xs, final, g_mix, w_in, w_out, conv_w, conv_b, ln_g, ln_b, dconv_w,
               hist_c, hist_d, ffn_inputs, next_weights):
    li = layer // 2
    tile, spt, seg = geo.tile, geo.seqs_per_tile, geo.sample_len
    in_w = w_in.shape[1]
    cwid = w_out.shape[0] // 2
    kc, kd = conv_w.shape[0], dconv_w.shape[0]
    hc, hd = kc - 1, kd - 1
    pc, pd = hist_c.shape[2], hist_d.shape[2]
    n_sample_seq = hist_c.shape[1]
    sample_tile, prompt_seq = _segment_maps(geo)

    mixer_specs = [
        _resident((1, geo.d_model)),
        _resident(w_in.shape),
        _resident(w_out.shape),
        _resident(conv_w.shape),
        _resident((1, cwid)),
        _resident((1, cwid)),
        _resident((1, cwid)),
        _resident(dconv_w.shape),
        pl.BlockSpec((None, spt, pc, cwid), lambda s: (li, sample_tile(s), 0, 0)),
        pl.BlockSpec((None, spt, pd, cwid), lambda s: (li, sample_tile(s), 0, 0)),
    ]
    out_shapes = [
        jax.ShapeDtypeStruct((geo.n_prompt_seq + 1, hc, cwid), F32),
        jax.ShapeDtypeStruct((geo.n_prompt_seq + 1, hd, cwid), F32),
        jax.ShapeDtypeStruct((n_sample_seq, hc, cwid), F32),
        jax.ShapeDtypeStruct((n_sample_seq, hd, cwid), F32),
    ]
    out_specs = [
        pl.BlockSpec((1, hc, cwid), lambda s: (prompt_seq(s), 0, 0)),
        pl.BlockSpec((1, hd, cwid), lambda s: (prompt_seq(s), 0, 0)),
        pl.BlockSpec((spt, hc, cwid), lambda s: (sample_tile(s), 0, 0)),
        pl.BlockSpec((spt, hd, cwid), lambda s: (sample_tile(s), 0, 0)),
    ]
    scratch = [
        ((spt, pc + seg, cwid), F32),
        ((spt, pd + seg, cwid), F32),
        ((pc, cwid), F32),
        ((pd, cwid), F32),
        ((tile, cwid), F32),
        ((tile, 2 * cwid), BF16),
    ]
    outs = _layer_call(
        geo, ("odd", layer), _odd_layer_kernel, xs, final,
        (g_mix, w_in, w_out, conv_w, conv_b, ln_g, ln_b, dconv_w, hist_c, hist_d), mixer_specs, ffn_inputs,
        out_shapes, out_specs, scratch, in_w, next_weights)
    n_rows = 2 if final else 1
    c_p, d_p, c_s, d_s = outs[n_rows:n_rows + 4]
    return outs[:n_rows], c_p[:geo.n_prompt_seq], d_p[:geo.n_prompt_seq], c_s, d_s, outs[n_rows + 4:]


def _gating_operands(w_s, b_s, sample_len):
    heads, blk, _ = w_s.shape
    cidx = jnp.arange(blk) // CHUNK
    mask = (cidx[None, :] <= cidx[:, None]).astype(w_s.dtype)
    w_prompt = w_s * mask[None]
    head_dim_lanes = LANES // 2
    reps = blk // sample_len
    eye = jnp.eye(reps, dtype=w_s.dtype)
    w_first = w_prompt[:, :sample_len, :sample_len]
    w_sample = jnp.einsum("ab,hij->haibj", eye, w_first).reshape(heads, blk, blk)
    b_sample = jnp.tile(b_s[:, :sample_len], (1, reps))

    def pairs(w):
        return jnp.concatenate([w[0::2], w[1::2]], axis=2)

    def bias(b):
        return jnp.repeat(b.T, head_dim_lanes, axis=1)

    wmix = jnp.stack([pairs(w_prompt), pairs(w_sample)]).astype(BF16)
    bmix = jnp.stack([bias(b_s), bias(b_sample)]).astype(F32)
    return wmix, bmix


def _pad_history(state):
    rows = state.shape[2]
    return jnp.pad(state, ((0, 0), (0, 0), (_round_up(rows, SUBLANES) - rows, 0), (0, 0)))


def kernel(x_prompt, x_sample, state_pool, state_conv_c, state_conv_d, norm_mix_g, norm_ffn_g, final_norm_g,
           w_in_even, w_out_even, a_w_s, a_b_s, a_ln_g, a_ln_b, b_pool_w, b_scale,
           w_in_odd, w_out_odd, c_conv_w, c_conv_b, c_ln_g, c_ln_b, d_conv_w, w_ff1, w_ff2):
    n_prompt_seq, seq, d = x_prompt.shape
    n_sample_seq, sample_len, _ = x_sample.shape
    depth = norm_mix_g.shape[0]
    tile = ROW_TILE
    assert seq % tile == 0 and tile % A_BLOCK == 0 and tile % sample_len == 0
    assert A_BLOCK % sample_len == 0 and sample_len <= CHUNK
    assert (n_sample_seq * sample_len) % tile == 0
    assert a_w_s.shape[1] == A_HEADS and a_w_s.shape[2] == A_BLOCK
    assert w_out_even.shape[1] // 2 == A_HEADS * (LANES // 2)
    assert state_pool.shape[2] == POOL_HIST and sample_len >= _round_up(c_conv_w.shape[1] - 1, SUBLANES)
    geo = Geometry(
        d_model=d, tile=tile,
        prompt_tiles=n_prompt_seq * seq // tile, tiles_per_seq=seq // tile, n_prompt_seq=n_prompt_seq,
        sample_tiles=n_sample_seq * sample_len // tile, sample_len=sample_len,
        seqs_per_tile=tile // sample_len)

    def row(v):
        return v.reshape(1, -1)

    pool_w_b = b_pool_w.astype(BF16)
    hist_pool, hist_c, hist_d = _pad_history(state_pool), _pad_history(state_conv_c), _pad_history(state_conv_d)

    def layer_weights(layer):
        li = layer // 2
        w_in, w_out = (w_in_even, w_out_even) if layer % 2 == 0 else (w_in_odd, w_out_odd)
        return (w_in, li), (w_out, li), (w_ff1, layer), (w_ff2, layer)

    weights_b = [w[index].astype(BF16) for w, index in layer_weights(0)]
    xs = (x_prompt.reshape(-1, d), x_sample.reshape(-1, d))
    pools_p, pools_s, vs, cs_p, cs_s, ds_p, ds_s = [], [], [], [], [], [], []
    for layer in range(depth):
        li = layer // 2
        final = layer == depth - 1
        w_in_b, w_out_b, w1_b, w2_b = weights_b
        next_weights = () if final else layer_weights(layer + 1)
        ffn_inputs = (row(norm_ffn_g[layer]), w1_b, w2_b, row(final_norm_g))
        if layer % 2 == 0:
            wmix, bmix = _gating_operands(a_w_s[li], a_b_s[li], sample_len)
            xs, v_s, pool_p, pool_s, weights_b = _even_layer(
                geo, layer, xs, final, row(norm_mix_g[layer]), w_in_b, w_out_b, wmix, bmix,
                row(a_ln_g[li]), row(a_ln_b[li]), pool_w_b[li], row(b_scale[li]), hist_pool, ffn_inputs,
                next_weights)
            vs.append(v_s.reshape(n_sample_seq, sample_len, -1))
            pools_p.append(pool_p)
            pools_s.append(pool_s)
        else:
            xs, c_p, d_p, c_s, d_s, weights_b = _odd_layer(
                geo, layer, xs, final, row(norm_mix_g[layer]), w_in_b, w_out_b, c_conv_w[li],
                row(c_conv_b[li]), row(c_ln_g[li]), row(c_ln_b[li]), d_conv_w[li], hist_c, hist_d, ffn_inputs,
                next_weights)
            cs_p.append(c_p)
            ds_p.append(d_p)
            cs_s.append(c_s)
            ds_s.append(d_s)

    y_prompt = xs[0].reshape(n_prompt_seq, seq, d)
    y_sample = xs[1].reshape(n_sample_seq, sample_len, d)
    return (y_prompt, y_sample, jnp.stack(pools_p), jnp.stack(cs_p), jnp.stack(ds_p),
            jnp.stack(vs), jnp.stack(pools_s), jnp.stack(cs_s), jnp.stack(ds_s))
```

```python
import functools
from typing import NamedTuple

import jax
import jax.numpy as jnp
from jax import lax
from jax.experimental import pallas as pl
from jax.experimental.pallas import tpu as pltpu

F32 = jnp.float32
BF16 = jnp.bfloat16

EPS = 1e-6
PAST_LEN = 4096
CHUNK = 64
A_BLOCK = 128
A_HEADS = 8
POOL_WINDOWS = (2, 4, 8, 16)
LANES = 128
SUBLANES = 8
VMEM_CAPACITY_BYTES = 64 * 1024 * 1024
ROW_TILE = 512
FF_CHUNK = 1024
CONV_ROWS = 64
NORM_ROWS = 64
CAST_STEPS = 32

POOL_HIST = max(POOL_WINDOWS) - 1


def _round_up(n, m):
    return -(-n // m) * m


POOL_PAD = _round_up(POOL_HIST, SUBLANES)


class Geometry(NamedTuple):
    d_model: int
    tile: int
    prompt_tiles: int
    tiles_per_seq: int
    n_prompt_seq: int
    sample_tiles: int
    sample_len: int
    seqs_per_tile: int

    @property
    def tiles(self):
        return self.prompt_tiles + self.sample_tiles

    @property
    def steps(self):
        return self.tiles + 1

    def mix_tile(self, s):
        return jnp.minimum(s, self.tiles - 1)

    def ffn_tile(self, s):
        return jnp.maximum(s - 1, 0)


def _resident(shape, index=None):
    full = (0,) * len(shape) if index is None else tuple(index)
    return pl.BlockSpec(shape, lambda s: full, pipeline_mode=pl.Buffered(1))


def _spec_bytes(spec, dtype):
    return _nbytes([b for b in spec.block_shape if b is not None], dtype)


def _nbytes(shape, dtype):
    n = 1
    for s in shape:
        n *= s
    return n * jnp.dtype(dtype).itemsize


def _vmem_limit(block_bytes, resident_bytes, scratch_bytes, temp_bytes):
    need = 2 * block_bytes + resident_bytes + scratch_bytes + temp_bytes
    assert need <= VMEM_CAPACITY_BYTES, need
    return min(VMEM_CAPACITY_BYTES - (2 << 20), _round_up(need + need // 8, 1 << 20))


def _rms_norm(x, g):
    y = x * lax.rsqrt(jnp.mean(x * x, axis=-1, keepdims=True) + EPS)
    return y * g


def _layer_norm(x, g, b):
    mu = jnp.mean(x, axis=-1, keepdims=True)
    xc = x - mu
    y = xc * lax.rsqrt(jnp.mean(xc * xc, axis=-1, keepdims=True) + EPS)
    return y * g + b


def _gelu_tanh(x):
    c = 0.7978845608028654
    return x * (0.5 * (1.0 + jnp.tanh(c * (x + 0.044715 * (x * x * x)))))


def _load_rows(x_refs, is_prompt):
    if len(x_refs) == 1:
        return x_refs[0][...]
    return jnp.where(is_prompt, x_refs[0][...], x_refs[1][...])


def _tile_ids(geo):
    step = pl.program_id(0)
    m = geo.mix_tile(step)
    return step, m < geo.prompt_tiles, m % geo.tiles_per_seq


def _derived_zero(x):
    u = pltpu.bitcast(x, jnp.uint32)
    u = lax.shift_right_logical(lax.shift_right_logical(u, jnp.uint32(16)), jnp.uint32(16))
    return u.astype(F32)


def _order_after(ref, token, lane_tiles=(0,)):
    rows = SUBLANES * (4 // jnp.dtype(ref.dtype).itemsize)
    z = _derived_zero(token)
    z = jnp.concatenate([z] * (rows // SUBLANES), axis=0).astype(ref.dtype)
    for lt in lane_tiles:
        idx = (slice(0, rows), slice(lt * LANES, (lt + 1) * LANES))
        ref[idx] = ref[idx] + z


def _ffn_stages(x1_ref, g_ref, w1_ref, w2_ref, acc_ref, h_ref, t_ref, dst_ref=None):
    n_chunks = w1_ref.shape[1] // FF_CHUNK

    def up(c):
        def run(after=None):
            if c == 0:
                for r0 in range(0, x1_ref.shape[0], NORM_ROWS):
                    rows = slice(r0, r0 + NORM_ROWS)
                    x = x1_ref[rows, :]
                    h_ref[rows, :] = _rms_norm(x, g_ref[...]).astype(BF16)
                    acc_ref[rows, :] = x
            if after is not None:
                _order_after(h_ref, after)
            cols = slice(c * FF_CHUNK, (c + 1) * FF_CHUNK)
            t = jnp.maximum(jnp.dot(h_ref[...], w1_ref[:, cols], preferred_element_type=F32), 0.0)
            t_ref[...] = (t * t).astype(BF16)
            return t[-SUBLANES:, -LANES:]
        return run

    def down(c):
        def run(after=None):
            if after is not None:
                _order_after(t_ref, after)
            cols = slice(c * FF_CHUNK, (c + 1) * FF_CHUNK)
            d = jnp.dot(t_ref[...], w2_ref[cols, :], preferred_element_type=F32)
            if dst_ref is not None and c == n_chunks - 1:
                dst_ref[...] = acc_ref[...] + d
            else:
                acc_ref[...] += d
            return d[-SUBLANES:, -LANES:]
        return run

    return [stage(c) for c in range(n_chunks) for stage in (up, down)]


def _run_interleaved(*stage_lists):
    order = sorted(((k + 0.5) / len(stages), n, k) for n, stages in enumerate(stage_lists) for k in range(len(stages)))
    for _, n, k in order:
        stage_lists[n][k]()


def _project(x_refs, is_prompt, g_ref, win_ref, xres_ref, z_ref, hm_ref):
    for r0 in range(0, xres_ref.shape[0], NORM_ROWS):
        rows = slice(r0, r0 + NORM_ROWS)
        x = (x_refs[0][rows, :] if len(x_refs) == 1
             else jnp.where(is_prompt, x_refs[0][rows, :], x_refs[1][rows, :]))
        xres_ref[rows, :] = x
        hm_ref[rows, :] = _rms_norm(x, g_ref[...]).astype(BF16)
    z_ref[...] = jnp.dot(hm_ref[...], win_ref[...], preferred_element_type=F32)


def _emit_rows(geo, step, final, acc_ref, gfin_ref, out_refs):
    if not final:
        return
    is_prompt = geo.ffn_tile(step) < geo.prompt_tiles

    @pl.when(is_prompt)
    def _():
        out_refs[0][...] = _rms_norm(acc_ref[...], gfin_ref[...])

    @pl.when(jnp.logical_not(is_prompt))
    def _():
        out_refs[1][...] = _rms_norm(acc_ref[...], gfin_ref[...])


def _pool_segment(ext, length, pos0, out_ref, out_base):
    pos = pos0 + lax.broadcasted_iota(jnp.int32, (length, 1), 0)
    for g, w in enumerate(POOL_WINDOWS):
        lanes = slice(g * LANES, (g + 1) * LANES)
        win = ext[0:POOL_PAD + length, lanes]
        s = win
        span = 1
        while span < w:
            s = s + pltpu.roll(s, span, axis=0)
            span *= 2
        inv_cnt = 1.0 / jnp.minimum(w, pos + 1).astype(F32)
        out_ref[out_base:out_base + length, lanes] = s[POOL_PAD:] * inv_cnt - win[POOL_PAD:]


def _cast_next_weights(refs, n_in, n_outs, n_cast):
    src = refs[n_in:n_in + n_cast]
    dst = refs[n_in + n_cast + n_outs:n_in + n_cast + n_outs + n_cast]
    for i_ref, o_ref in zip(src, dst):
        o_ref[...] = i_ref[...].astype(BF16)
    return refs[:n_in] + refs[n_in + n_cast:n_in + n_cast + n_outs] + refs[n_in + 2 * n_cast + n_outs:]


def _even_layer_kernel(geo, n_x, final, n_cast, *refs):
    n_out = 2 if final else 1
    refs = _cast_next_weights(refs, n_x + 14, n_out + 3, n_cast)
    x_refs = refs[:n_x]
    (g_ref, win_ref, wout_ref, wmix_ref, bmix_ref, lng_ref, lnb_ref, pw_ref, bscale_ref, hist_ref,
     gf_ref, w1_ref, w2_ref, gfin_ref) = refs[n_x:n_x + 14]
    out_refs = refs[n_x + 14:n_x + 14 + n_out]
    (v_ref, poolp_ref, pools_ref,
     x1_ref, acc_ref, z_ref, xres_ref, h_ref, t_ref, hm_ref, ext_ref, carry_ref, pooled_ref,
     ycat_ref) = refs[n_x + 14 + n_out:]
    step, is_prompt, j = _tile_ids(geo)
    tile, seg = geo.tile, geo.sample_len
    aw = wout_ref.shape[0] // 2

    @pl.when(step == 0)
    def _():
        x1_ref[...] = jnp.zeros(x1_ref.shape, F32)
        carry_ref[...] = jnp.zeros(carry_ref.shape, F32)

    ffn = _ffn_stages(x1_ref, gf_ref, w1_ref, w2_ref, acc_ref, h_ref, t_ref, None if final else out_refs[0])
    _project(x_refs, is_prompt, g_ref, win_ref, xres_ref, z_ref, hm_ref)

    fresh = j == 0

    def pool_stage(s):
        def run():
            rows = slice(s * seg, (s + 1) * seg)
            if s == 0:
                before = jnp.where(fresh, 0.0, carry_ref[...])
            else:
                before = z_ref[s * seg - POOL_PAD:s * seg, 2 * aw:]
            ext_ref[s, 0:POOL_PAD, :] = jnp.where(is_prompt, before, hist_ref[s])
            ext_ref[s, POOL_PAD:, :] = z_ref[rows, 2 * aw:]
            pos0 = jnp.where(is_prompt, j * tile + s * seg, PAST_LEN)
            _pool_segment(ext_ref.at[s], seg, pos0, pooled_ref, s * seg)
        return run

    lane = lax.broadcasted_iota(jnp.int32, (A_BLOCK, LANES), 1)
    low_head = lane < (LANES // 2)
    n_pairs = aw // LANES

    def gate_stage(b):
        def run():
            blocks = (slice(b * A_BLOCK, (b + 1) * A_BLOCK), slice((b + 1) * A_BLOCK, (b + 2) * A_BLOCK))
            vs = []
            for rows in blocks:
                v = _layer_norm(_gelu_tanh(z_ref[rows, aw:2 * aw]), lng_ref[...], lnb_ref[...])
                v_ref[rows, :] = v
                vs.append(v)
            mixed = ([], [])
            for p in range(n_pairs):
                rhs = []
                for v in vs:
                    vp = v[:, p * LANES:(p + 1) * LANES].astype(BF16)
                    zero = jnp.zeros_like(vp)
                    rhs.append(jnp.concatenate([jnp.where(low_head, vp, zero), jnp.where(low_head, zero, vp)],
                                               axis=0))
                both = jnp.dot(wmix_ref[0, p], jnp.concatenate(rhs, axis=1), preferred_element_type=F32)
                mixed[0].append(both[:, :LANES])
                mixed[1].append(both[:, LANES:])
            for rows, parts in zip(blocks, mixed):
                u = _gelu_tanh(z_ref[rows, 0:aw])
                ycat_ref[rows, 0:aw] = (u * (jnp.concatenate(parts, axis=1) + bmix_ref[0])).astype(BF16)
        return run

    mixer = ([pool_stage(s) for s in range(geo.seqs_per_tile)]
             + [gate_stage(b) for b in range(0, tile // A_BLOCK, 2)])
    _run_interleaved(mixer, ffn)
    carry_ref[...] = z_ref[tile - POOL_PAD:tile, 2 * aw:]

    for g in range(len(POOL_WINDOWS)):
        lanes = slice(g * LANES, (g + 1) * LANES)
        yb = jnp.dot(pooled_ref[:, lanes].astype(BF16), pw_ref[g], preferred_element_type=F32)
        ycat_ref[:, aw + g * LANES:aw + (g + 1) * LANES] = (yb * bscale_ref[:, lanes]).astype(BF16)

    y = jnp.dot(ycat_ref[...], wout_ref[...], preferred_element_type=F32)
    x1_ref[...] = xres_ref[...] + y

    tail = slice(POOL_PAD + seg - POOL_HIST, POOL_PAD + seg)
    for s in range(geo.seqs_per_tile):
        pools_ref[s] = ext_ref[s, tail, :]
    poolp_ref[0] = ext_ref[geo.seqs_per_tile - 1, tail, :]
    _emit_rows(geo, step, final, acc_ref, gfin_ref, out_refs)


def _dwconv_rows(ext, first_row, w_ref, n_taps, r0, length, lanes):
    wlen = length + _round_up(first_row + n_taps - 1, SUBLANES)
    win = ext[r0:r0 + wlen, lanes]
    acc = None
    for r in range(SUBLANES):
        taps = [k for k in range(n_taps) if (first_row + k) % SUBLANES == r]
        if not taps:
            continue
        shifted = win if r == 0 else pltpu.roll(win, wlen - r, axis=0)
        for k in taps:
            q0 = (first_row + k) // SUBLANES * SUBLANES
            term = shifted[q0:q0 + length] * w_ref[k:k + 1, lanes]
            acc = term if acc is None else acc + term
    return acc


def _odd_layer_kernel(geo, n_x, final, n_cast, *refs):
    n_out = 2 if final else 1
    refs = _cast_next_weights(refs, n_x + 14, n_out + 4, n_cast)
    x_refs = refs[:n_x]
    (g_ref, win_ref, wout_ref, cw_ref, cb_ref, lng_ref, lnb_ref, dw_ref, histc_ref, histd_ref,
     gf_ref, w1_ref, w2_ref, gfin_ref) = refs[n_x:n_x + 14]
    out_refs = refs[n_x + 14:n_x + 14 + n_out]
    (cp_ref, dp_ref, cs_ref, ds_ref,
     x1_ref, acc_ref, z_ref, xres_ref, h_ref, t_ref, hm_ref, extc_ref, extd_ref, carryc_ref, carryd_ref, conv_ref,
     ycat_ref) = refs[n_x + 14 + n_out:]
    step, is_prompt, j = _tile_ids(geo)
    tile, seg = geo.tile, geo.sample_len
    cwid = wout_ref.shape[0] // 2
    kc, kd = cw_ref.shape[0], dw_ref.shape[0]
    hc, hd = kc - 1, kd - 1
    pc, pd = carryc_ref.shape[0], carryd_ref.shape[0]
    o = 2 * cwid

    @pl.when(step == 0)
    def _():
        x1_ref[...] = jnp.zeros(x1_ref.shape, F32)
        carryc_ref[...] = jnp.zeros(carryc_ref.shape, F32)
        carryd_ref[...] = jnp.zeros(carryd_ref.shape, F32)

    ffn = _ffn_stages(x1_ref, gf_ref, w1_ref, w2_ref, acc_ref, h_ref, t_ref, None if final else out_refs[0])
    _project(x_refs, is_prompt, g_ref, win_ref, xres_ref, z_ref, hm_ref)

    fresh = j == 0

    def conv_stage(s):
        def run(after=None):
            rows = slice(s * seg, (s + 1) * seg)
            c_in = z_ref[rows, 0:cwid] * jax.nn.sigmoid(z_ref[rows, cwid:2 * cwid])
            d_in = z_ref[rows, o + cwid:o + 2 * cwid] * z_ref[rows, o + 2 * cwid:o + 3 * cwid]
            if s == 0:
                prev_c = jnp.where(fresh, 0.0, carryc_ref[...])
                prev_d = jnp.where(fresh, 0.0, carryd_ref[...])
            else:
                prev_c = extc_ref[s - 1, seg:, :]
                prev_d = extd_ref[s - 1, seg:, :]
            extc_ref[s, 0:pc, :] = jnp.where(is_prompt, prev_c, histc_ref[s])
            extd_ref[s, 0:pd, :] = jnp.where(is_prompt, prev_d, histd_ref[s])
            extc_ref[s, pc:, :] = c_in
            extd_ref[s, pd:, :] = d_in
            extc, extd = extc_ref.at[s], extd_ref.at[s]
            if after is not None:
                _order_after(extc, after, range(cwid // LANES))
                _order_after(extd, after, range(cwid // LANES))
            cacc = None
            for lt in range(cwid // LANES):
                lanes = slice(lt * LANES, (lt + 1) * LANES)
                for r0 in range(0, seg, CONV_ROWS):
                    out_rows = slice(s * seg + r0, s * seg + r0 + CONV_ROWS)
                    cacc = _dwconv_rows(extc, pc - hc, cw_ref, kc, r0, CONV_ROWS, lanes)
                    conv_ref[out_rows, lanes] = cacc + cb_ref[:, lanes]
                    d = _dwconv_rows(extd, pd - hd, dw_ref, kd, r0, CONV_ROWS, lanes)
                    ycat_ref[out_rows, cwid + lt * LANES:cwid + (lt + 1) * LANES] = (
                        z_ref[out_rows, o + lt * LANES:o + (lt + 1) * LANES] * d).astype(BF16)
            return cacc[:SUBLANES]
        return run

    def norm_stage(b):
        def run():
            rows = slice(b * A_BLOCK, (b + 1) * A_BLOCK)
            c = _layer_norm(conv_ref[rows, :], lng_ref[...], lnb_ref[...])
            ycat_ref[rows, 0:cwid] = (c * jax.nn.sigmoid(c)).astype(BF16)
        return run

    assert len(ffn) == geo.seqs_per_tile
    tok_f, tok_c = None, None
    for k in range(len(ffn)):
        new_f = ffn[k](after=tok_c)
        tok_c = conv_stage(k)(after=tok_f)
        tok_f = new_f
    for b in range(tile // A_BLOCK):
        norm_stage(b)()
    last = geo.seqs_per_tile - 1
    carryc_ref[...] = extc_ref[last, seg:, :]
    carryd_ref[...] = extd_ref[last, seg:, :]

    y = jnp.dot(ycat_ref[...], wout_ref[...], preferred_element_type=F32)
    x1_ref[...] = xres_ref[...] + y

    tail_c, tail_d = slice(pc + seg - hc, pc + seg), slice(pd + seg - hd, pd + seg)
    for s in range(geo.seqs_per_tile):
        cs_ref[s] = extc_ref[s, tail_c, :]
        ds_ref[s] = extd_ref[s, tail_d, :]
    cp_ref[0] = extc_ref[last, tail_c, :]
    dp_ref[0] = extd_ref[last, tail_d, :]
    _emit_rows(geo, step, final, acc_ref, gfin_ref, out_refs)


def _layer_call(geo, name, body, xs, final, mixer_inputs, mixer_specs, ffn_inputs,
                mixer_out_shapes, mixer_out_specs, mixer_scratch, in_w, next_weights):
    d, tile, pt = geo.d_model, geo.tile, geo.prompt_tiles
    g_ffn, w1, w2, g_final = ffn_inputs
    assert geo.steps >= CAST_STEPS
    cast_in_specs, cast_specs, cast_shapes = [], [], []
    for w, index in next_weights:
        n_rows, cols = w.shape[1:]
        rows = n_rows // CAST_STEPS
        assert n_rows % CAST_STEPS == 0 and rows % (2 * SUBLANES) == 0, w.shape
        cast_in_specs.append(pl.BlockSpec((None, rows, cols),
                                          lambda s, index=index: (index, jnp.minimum(s, CAST_STEPS - 1), 0)))
        cast_specs.append(pl.BlockSpec((rows, cols), lambda s: (jnp.minimum(s, CAST_STEPS - 1), 0)))
        cast_shapes.append(jax.ShapeDtypeStruct((n_rows, cols), BF16))
    ffn_tile = geo.ffn_tile
    if len(xs) == 1:
        x_specs = [pl.BlockSpec((tile, d), lambda s: (geo.mix_tile(s), 0))]
    else:
        x_specs = [pl.BlockSpec((tile, d), lambda s: (jnp.minimum(s, pt - 1), 0)),
                   pl.BlockSpec((tile, d), lambda s: (jnp.clip(s - pt, 0, geo.sample_tiles - 1), 0))]
    ffn_specs = [_resident((1, d)), _resident(w1.shape), _resident(w2.shape), _resident((1, d))]
    if final:
        row_shapes = [jax.ShapeDtypeStruct((pt * tile, d), F32),
                      jax.ShapeDtypeStruct((geo.sample_tiles * tile, d), F32)]
        row_specs = [pl.BlockSpec((tile, d), lambda s: (jnp.minimum(ffn_tile(s), pt - 1), 0)),
                     pl.BlockSpec((tile, d), lambda s: (jnp.maximum(ffn_tile(s) - pt, 0), 0))]
    else:
        row_shapes = [jax.ShapeDtypeStruct((geo.tiles * tile, d), F32)]
        row_specs = [pl.BlockSpec((tile, d), lambda s: (ffn_tile(s), 0))]
    scratch = [((tile, d), F32),
               ((tile, d), F32),
               ((tile, in_w), F32),
               ((tile, d), F32),
               ((tile, d), BF16),
               ((tile, FF_CHUNK), BF16),
               ((tile, d), BF16)] + mixer_scratch
    mixer_io = list(zip(mixer_specs, mixer_inputs)) + list(zip(mixer_out_specs, mixer_out_shapes))
    block_bytes = ((len(xs) + len(row_shapes)) * _nbytes((tile, d), F32)
                   + sum(_spec_bytes(sp, a.dtype) for sp, a in mixer_io if sp.pipeline_mode is None)
                   + sum(_spec_bytes(sp, F32) + _spec_bytes(sp, BF16) for sp in cast_specs))
    resident_bytes = (_nbytes(w1.shape, w1.dtype) + _nbytes(w2.shape, w2.dtype)
                      + sum(_spec_bytes(sp, a.dtype) for sp, a in mixer_io if sp.pipeline_mode is not None))
    scratch_bytes = sum(_nbytes(s, t) for s, t in scratch)
    temp_bytes = _nbytes((tile, FF_CHUNK), F32) + 2 * _nbytes((tile, d), F32)

    return pl.pallas_call(
        functools.partial(body, geo, len(xs), final, len(next_weights)),
        grid=(geo.steps,),
        in_specs=x_specs + mixer_specs + ffn_specs + cast_in_specs,
        out_specs=row_specs + mixer_out_specs + cast_specs,
        out_shape=row_shapes + mixer_out_shapes + cast_shapes,
        scratch_shapes=[pltpu.VMEM(s, t) for s, t in scratch],
        compiler_params=pltpu.CompilerParams(
            dimension_semantics=("arbitrary",),
            vmem_limit_bytes=_vmem_limit(block_bytes, resident_bytes, scratch_bytes, temp_bytes)),
        name=f"{name[0]}_layer_{name[1]}",
    )(*xs, *mixer_inputs, g_ffn, w1, w2, g_final, *(w for w, _ in next_weights))


def _segment_maps(geo):
    pt = geo.prompt_tiles

    def sample_tile(s):
        return jnp.maximum(geo.mix_tile(s) - pt, 0)

    def prompt_seq(s):
        i = geo.mix_tile(s)
        return jnp.where(i < pt, i // geo.tiles_per_seq, geo.n_prompt_seq)

    return sample_tile, prompt_seq


def _even_layer(geo, layer, xs, final, g_mix, w_in, w_out, wmix, bmix, ln_g, ln_b, pool_w, b_scale,
                hist_pool, ffn_inputs, next_weights):
    li = layer // 2
    tile, spt, seg, pt = geo.tile, geo.seqs_per_tile, geo.sample_len, geo.prompt_tiles
    in_w = w_in.shape[1]
    aw = w_out.shape[0] // 2
    bw = in_w - 2 * aw
    n_sample_seq = hist_pool.shape[1]
    sample_tile, prompt_seq = _segment_maps(geo)

    mixer_specs = [
        _resident((1, geo.d_model)),
        _resident(w_in.shape),
        _resident(w_out.shape),
        pl.BlockSpec((1,) + wmix.shape[1:], lambda s: (geo.mix_tile(s) // pt, 0, 0, 0)),
        pl.BlockSpec((1,) + bmix.shape[1:], lambda s: (geo.mix_tile(s) // pt, 0, 0)),
        _resident((1, aw)),
        _resident((1, aw)),
        _resident(pool_w.shape),
        _resident((1, bw)),
        pl.BlockSpec((None, spt, POOL_PAD, bw), lambda s: (li, sample_tile(s), 0, 0)),
    ]
    out_shapes = [
        jax.ShapeDtypeStruct((geo.sample_tiles * tile, aw), F32),
        jax.ShapeDtypeStruct((geo.n_prompt_seq + 1, POOL_HIST, bw), F32),
        jax.ShapeDtypeStruct((n_sample_seq, POOL_HIST, bw), F32),
    ]
    out_specs = [
        pl.BlockSpec((tile, aw), lambda s: (sample_tile(s), 0)),
        pl.BlockSpec((1, POOL_HIST, bw), lambda s: (prompt_seq(s), 0, 0)),
        pl.BlockSpec((spt, POOL_HIST, bw), lambda s: (sample_tile(s), 0, 0)),
    ]
    scratch = [
        ((spt, POOL_PAD + seg, bw), F32),
        ((POOL_PAD, bw), F32),
        ((tile, bw), F32),
        ((tile, aw + bw), BF16),
    ]
    outs = _layer_call(
        geo, ("even", layer), _even_layer_kernel, xs, final,
        (g_mix, w_in, w_out, wmix, bmix, ln_g, ln_b, pool_w, b_scale, hist_pool), mixer_specs, ffn_inputs,
        out_shapes, out_specs, scratch, in_w, next_weights)
    n_rows = 2 if final else 1
    v_s, pool_p, pool_s = outs[n_rows:n_rows + 3]
    return outs[:n_rows], v_s, pool_p[:geo.n_prompt_seq], pool_s, outs[n_rows + 3:]


def _odd_layer(geo, layer, xs, final, g_mix, w_in, w_out, conv_w, conv_b, ln_g, ln_b, dconv_w,
               hist_c, hist_d, ffn_inputs, next_weights):
    li = layer // 2
    tile, spt, seg = geo.tile, geo.seqs_per_tile, geo.sample_len
    in_w = w_in.shape[1]
    cwid = w_out.shape[0] // 2
    kc, kd = conv_w.shape[0], dconv_w.shape[0]
    hc, hd = kc - 1, kd - 1
    pc, pd = hist_c.shape[2], hist_d.shape[2]
    n_sample_seq = hist_c.shape[1]
    sample_tile, prompt_seq = _segment_maps(geo)

    mixer_specs = [
        _resident((1, geo.d_model)),
        _resident(w_in.shape),
        _resident(w_out.shape),
        _resident(conv_w.shape),
        _resident((1, cwid)),
        _resident((1, cwid)),
        _resident((1, cwid)),
        _resident(dconv_w.shape),
        pl.BlockSpec((None, spt, pc, cwid), lambda s: (li, sample_tile(s), 0, 0)),
        pl.BlockSpec((None, spt, pd, cwid), lambda s: (li, sample_tile(s), 0, 0)),
    ]
    out_shapes = [
        jax.ShapeDtypeStruct((geo.n_prompt_seq + 1, hc, cwid), F32),
        jax.ShapeDtypeStruct((geo.n_prompt_seq + 1, hd, cwid), F32),
        jax.ShapeDtypeStruct((n_sample_seq, hc, cwid), F32),
        jax.ShapeDtypeStruct((n_sample_seq, hd, cwid), F32),
    ]
    out_specs = [
        pl.BlockSpec((1, hc, cwid), lambda s: (prompt_seq(s), 0, 0)),
        pl.BlockSpec((1, hd, cwid), lambda s: (prompt_seq(s), 0, 0)),
        pl.BlockSpec((spt, hc, cwid), lambda s: (sample_tile(s), 0, 0)),
        pl.BlockSpec((spt, hd, cwid), lambda s: (sample_tile(s), 0, 0)),
    ]
    scratch = [
        ((spt, pc + seg, cwid), F32),
        ((spt, pd + seg, cwid), F32),
        ((pc, cwid), F32),
        ((pd, cwid), F32),
        ((tile, cwid), F32),
        ((tile, 2 * cwid), BF16),
    ]
    outs = _layer_call(
        geo, ("odd", layer), _odd_layer_kernel, xs, final,
        (g_mix, w_in, w_out, conv_w, conv_b, ln_g, ln_b, dconv_w, hist_c, hist_d), mixer_specs, ffn_inputs,
        out_shapes, out_specs, scratch, in_w, next_weights)
    n_rows = 2 if final else 1
    c_p, d_p, c_s, d_s = outs[n_rows:n_rows + 4]
    return outs[:n_rows], c_p[:geo.n_prompt_seq], d_p[:geo.n_prompt_seq], c_s, d_s, outs[n_rows + 4:]


def _gating_operands(w_s, b_s, sample_len):
    heads, blk, _ = w_s.shape
    cidx = jnp.arange(blk) // CHUNK
    mask = (cidx[None, :] <= cidx[:, None]).astype(w_s.dtype)
    w_prompt = w_s * mask[None]
    head_dim_lanes = LANES // 2
    reps = blk // sample_len
    eye = jnp.eye(reps, dtype=w_s.dtype)
    w_first = w_prompt[:, :sample_len, :sample_len]
    w_sample = jnp.einsum("ab,hij->haibj", eye, w_first).reshape(heads, blk, blk)
    b_sample = jnp.tile(b_s[:, :sample_len], (1, reps))

    def pairs(w):
        return jnp.concatenate([w[0::2], w[1::2]], axis=2)

    def bias(b):
        return jnp.repeat(b.T, head_dim_lanes, axis=1)

    wmix = jnp.stack([pairs(w_prompt), pairs(w_sample)]).astype(BF16)
    bmix = jnp.stack([bias(b_s), bias(b_sample)]).astype(F32)
    return wmix, bmix


def _pad_history(state):
    rows = state.shape[2]
    return jnp.pad(state, ((0, 0), (0, 0), (_round_up(rows, SUBLANES) - rows, 0), (0, 0)))


def kernel(x_prompt, x_sample, state_pool, state_conv_c, state_conv_d, norm_mix_g, norm_ffn_g, final_norm_g,
           w_in_even, w_out_even, a_w_s, a_b_s, a_ln_g, a_ln_b, b_pool_w, b_scale,
           w_in_odd, w_out_odd, c_conv_w, c_conv_b, c_ln_g, c_ln_b, d_conv_w, w_ff1, w_ff2):
    n_prompt_seq, seq, d = x_prompt.shape
    n_sample_seq, sample_len, _ = x_sample.shape
    depth = norm_mix_g.shape[0]
    tile = ROW_TILE
    assert seq % tile == 0 and tile % A_BLOCK == 0 and tile % sample_len == 0
    assert A_BLOCK % sample_len == 0 and sample_len <= CHUNK
    assert (n_sample_seq * sample_len) % tile == 0
    assert a_w_s.shape[1] == A_HEADS and a_w_s.shape[2] == A_BLOCK
    assert w_out_even.shape[1] // 2 == A_HEADS * (LANES // 2)
    assert state_pool.shape[2] == POOL_HIST and sample_len >= _round_up(c_conv_w.shape[1] - 1, SUBLANES)
    geo = Geometry(
        d_model=d, tile=tile,
        prompt_tiles=n_prompt_seq * seq // tile, tiles_per_seq=seq // tile, n_prompt_seq=n_prompt_seq,
        sample_tiles=n_sample_seq * sample_len // tile, sample_len=sample_len,
        seqs_per_tile=tile // sample_len)

    def row(v):
        return v.reshape(1, -1)

    pool_w_b = b_pool_w.astype(BF16)
    hist_pool, hist_c, hist_d = _pad_history(state_pool), _pad_history(state_conv_c), _pad_history(state_conv_d)

    def layer_weights(layer):
        li = layer // 2
        w_in, w_out = (w_in_even, w_out_even) if layer % 2 == 0 else (w_in_odd, w_out_odd)
        return (w_in, li), (w_out, li), (w_ff1, layer), (w_ff2, layer)

    weights_b = [w[index].astype(BF16) for w, index in layer_weights(0)]
    xs = (x_prompt.reshape(-1, d), x_sample.reshape(-1, d))
    pools_p, pools_s, vs, cs_p, cs_s, ds_p, ds_s = [], [], [], [], [], [], []
    for layer in range(depth):
        li = layer // 2
        final = layer == depth - 1
        w_in_b, w_out_b, w1_b, w2_b = weights_b
        next_weights = () if final else layer_weights(layer + 1)
        ffn_inputs = (row(norm_ffn_g[layer]), w1_b, w2_b, row(final_norm_g))
        if layer % 2 == 0:
            wmix, bmix = _gating_operands(a_w_s[li], a_b_s[li], sample_len)
            xs, v_s, pool_p, pool_s, weights_b = _even_layer(
                geo, layer, xs, final, row(norm_mix_g[layer]), w_in_b, w_out_b, wmix, bmix,
                row(a_ln_g[li]), row(a_ln_b[li]), pool_w_b[li], row(b_scale[li]), hist_pool, ffn_inputs,
                next_weights)
            vs.append(v_s.reshape(n_sample_seq, sample_len, -1))
            pools_p.append(pool_p)
            pools_s.append(pool_s)
        else:
            xs, c_p, d_p, c_s, d_s, weights_b = _odd_layer(
                geo, layer, xs, final, row(norm_mix_g[layer]), w_in_b, w_out_b, c_conv_w[li],
                row(c_conv_b[li]), row(c_ln_g[li]), row(c_ln_b[li]), d_conv_w[li], hist_c, hist_d, ffn_inputs,
                next_weights)
            cs_p.append(c_p)
            ds_p.append(d_p)
            cs_s.append(c_s)
            ds_s.append(d_s)

    y_prompt = xs[0].reshape(n_prompt_seq, seq, d)
    y_sample = xs[1].reshape(n_sample_seq, sample_len, d)
    return (y_prompt, y_sample, jnp.stack(pools_p), jnp.stack(cs_p), jnp.stack(ds_p),
            jnp.stack(vs), jnp.stack(pools_s), jnp.stack(cs_s), jnp.stack(ds_s))
```
